```python
import math
import jax
import jax.numpy as jnp
from jax import lax
import numpy as np

D_MODEL = 1024
BATCH = 2
SEQ = 16384
DEPTH = 4

GRID_W = 64
CTX_LEN = 256
N_MIXERS = 4
NORM_EPS = 1e-6
ROPE_BASE = 10000.0

GDN_DK = 128
GDN_DV = 128
GDN_HEADS = D_MODEL // GDN_DV
GDN_CONV = 5
GDN_CHUNK = 64
GDN_PROJ = 2 * GDN_HEADS * GDN_DK + 2 * GDN_HEADS * GDN_DV + 4 * GDN_HEADS

MLSTM_DV = 128
MLSTM_DQK = MLSTM_DV // 2
MLSTM_HEADS = D_MODEL // MLSTM_DV
MLSTM_CHUNK = 64
MLSTM_PROJ = 2 * MLSTM_HEADS * MLSTM_DQK + 2 * MLSTM_HEADS * MLSTM_DV + 4 * MLSTM_HEADS

DIFF_DH = 64
DIFF_HEADS = D_MODEL // (2 * DIFF_DH)
DIFF_PROJ = 6 * DIFF_HEADS * DIFF_DH

SWA_DH = 64
SWA_QHEADS = D_MODEL // SWA_DH
SWA_KVHEADS = SWA_QHEADS // 4
SWA_WINDOW = 128
SWA_PROJ = (SWA_QHEADS + 2 * SWA_KVHEADS) * SWA_DH

ATTN_QBLOCK = 128

N_EXPERTS = 32
TOP_K = 4
D_FF = D_MODEL
SWIGLU_ALPHA = 1.702
SWIGLU_LIMIT = 7.0
MOE_BLOCK = 128

kernel_name = 'hybrid_interleaved_dit_moe_trunk'


def _layers_using(m):
    return len(range(m, DEPTH, N_MIXERS))


def _rmsnorm(x, g):
    xf = x.astype(jnp.float32)
    y = xf * lax.rsqrt(jnp.mean(xf * xf, axis=-1, keepdims=True) + NORM_EPS)
    return (y * g.astype(jnp.float32)).astype(x.dtype)


def _modulate(h, shift, scale):
    return h * (1 + scale) + shift


def _l2norm(t):
    tf = t.astype(jnp.float32)
    return tf * lax.rsqrt(jnp.sum(tf * tf, axis=-1, keepdims=True) + NORM_EPS)


def _heads(t, n):
    b, l, _ = t.shape
    return t.reshape(b, l, n, -1).transpose(0, 2, 1, 3)


def _axial_rope_tables(seq, head_dim):
    rows = seq // GRID_W
    row = jnp.repeat(jnp.arange(rows), GRID_W).astype(jnp.float32)
    col = jnp.tile(jnp.arange(GRID_W), rows).astype(jnp.float32)
    quarter = head_dim // 4
    inv = ROPE_BASE ** (-jnp.arange(quarter, dtype=jnp.float32) / quarter)
    ang_r = row[:, None] * inv[None, :]
    ang_c = col[:, None] * inv[None, :]
    return (jnp.cos(ang_r), jnp.sin(ang_r), jnp.cos(ang_c), jnp.sin(ang_c))


def _apply_axial_rope(t, tables):
    cr, sr, cc, sc = tables
    q = cr.shape[-1]
    tf = t.astype(jnp.float32)

    def rot(seg, cos, sin):
        a, b = seg[..., :q], seg[..., q:]
        return jnp.concatenate([a * cos - b * sin, b * cos + a * sin], axis=-1)

    out = jnp.concatenate([rot(tf[..., :2 * q], cr, sr), rot(tf[..., 2 * q:], cc, sc)], axis=-1)
    return out.astype(t.dtype)


def _centred_dwconv(t, w):
    k, ch = w.shape
    return lax.conv_general_dilated(t, w[:, None, :].astype(t.dtype), window_strides=(1,),
                                    padding=[(k // 2, k // 2)], dimension_numbers=('NWC', 'WIO', 'NWC'),
                                    feature_group_count=ch)


def _flip(t):
    return jnp.flip(t, axis=2)


def _bidirectional_prefix_scan(scan_fn, ctx_f, ctx_b, lat_f, lat_b, state0):
    oc_f, st_f = scan_fn(*ctx_f, state0)
    ox_f, _ = scan_fn(*lat_f, st_f)
    oc_b, st_b = scan_fn(*[_flip(t) for t in ctx_b], state0)
    ox_b, _ = scan_fn(*[_flip(t) for t in lat_b], st_b)
    return oc_f + _flip(oc_b), ox_f + _flip(ox_b)


def _gated_delta_chunked(q, k, v, beta, g, state):
    B, H, L, dk = q.shape
    dv = v.shape[-1]
    C = GDN_CHUNK
    n = L // C
    q = (q.astype(jnp.float32) * dk ** -0.5).reshape(B, H, n, C, dk)
    k = k.astype(jnp.float32).reshape(B, H, n, C, dk)
    v = v.astype(jnp.float32).reshape(B, H, n, C, dv)
    beta = beta.reshape(B, H, n, C)
    g = jnp.cumsum(g.reshape(B, H, n, C), axis=-1)
    tri = jnp.tril(jnp.ones((C, C), bool))
    stri = jnp.tril(jnp.ones((C, C), bool), -1)
    gam = jnp.exp(jnp.where(tri, g[..., :, None] - g[..., None, :], -jnp.inf))
    kb = k * beta[..., None]
    a_low = jnp.where(stri, jnp.einsum('bhnid,bhnjd->bhnij', kb, k) * gam, 0.0)
    rhs = jnp.concatenate([v * beta[..., None], kb * jnp.exp(g)[..., None]], axis=-1)
    sol = lax.linalg.triangular_solve(jnp.eye(C, dtype=jnp.float32) + a_low, rhs, left_side=True,
                                      lower=True, unit_diagonal=True)
    u, w = sol[..., :dv], sol[..., dv:]
    qk = jnp.einsum('bhnid,bhnjd->bhnij', q, k) * gam
    qg = q * jnp.exp(g)[..., None]
    kd = k * jnp.exp(g[..., -1:] - g)[..., None]
    g_end = jnp.exp(g[..., -1])

    def step(s, inp):
        u_c, w_c, qg_c, qk_c, kd_c, ge_c = inp
        v_new = u_c - jnp.einsum('bhck,bhkv->bhcv', w_c, s)
        o = jnp.einsum('bhck,bhkv->bhcv', qg_c, s) + jnp.einsum('bhcs,bhsv->bhcv', qk_c, v_new)
        s = s * ge_c[..., None, None] + jnp.einsum('bhck,bhcv->bhkv', kd_c, v_new)
        return s, o

    xs = tuple(jnp.moveaxis(t, 2, 0) for t in (u, w, qg, qk, kd, g_end))
    s_fin, o = lax.scan(step, state, xs)
    return jnp.moveaxis(o, 0, 2).reshape(B, H, L, dv), s_fin


def _mlstm_chunked(q, k, v, logi, logf, state):
    B, H, L, dqk = q.shape
    dv = v.shape[-1]
    C = MLSTM_CHUNK
    n = L // C
    q = q.astype(jnp.float32).reshape(B, H, n, C, dqk)
    k = (k.astype(jnp.float32) * dqk ** -0.5).reshape(B, H, n, C, dqk)
    v = v.astype(jnp.float32).reshape(B, H, n, C, dv)
    logi = logi.reshape(B, H, n, C)
    b = jnp.cumsum(logf.reshape(B, H, n, C), axis=-1)
    tri = jnp.tril(jnp.ones((C, C), bool))
    dlog = jnp.where(tri, b[..., :, None] - b[..., None, :] + logi[..., None, :], -jnp.inf)
    m_intra = jnp.max(dlog, axis=-1)
    b_last = b[..., -1]
    a_loc = b_last[..., None] - b + logi
    m_loc = jnp.max(a_loc, axis=-1)
    w_loc = jnp.exp(a_loc - m_loc[..., None])
    c_loc = jnp.einsum('bhnc,bhncv,bhnck->bhnvk', w_loc, v, k)
    n_loc = jnp.einsum('bhnc,bhnck->bhnk', w_loc, k)

    def step(carry, inp):
        c_st, n_st, m_st = carry
        cl, nl, ml, bl = inp
        m_new = jnp.maximum(bl + m_st, ml)
        sp = jnp.exp(bl + m_st - m_new)
        sl = jnp.exp(ml - m_new)
        new = (c_st * sp[..., None, None] + cl * sl[..., None, None], n_st * sp[..., None] + nl * sl[..., None], m_new)
        return new, (c_st, n_st, m_st)

    xs = tuple(jnp.moveaxis(t, 2, 0) for t in (c_loc, n_loc, m_loc, b_last))
    final, (c_in, n_in, m_in) = lax.scan(step, state, xs)
    c_in = jnp.moveaxis(c_in, 0, 2)
    n_in = jnp.moveaxis(n_in, 0, 2)
    m_in = jnp.moveaxis(m_in, 0, 2)
    log_inter = b + m_in[..., None]
    m_t = jnp.maximum(log_inter, m_intra)
    w_inter = jnp.exp(log_inter - m_t)
    pm = jnp.exp(dlog - m_t[..., None]) * jnp.einsum('bhntd,bhnsd->bhnts', q, k)
    num = w_inter[..., None] * jnp.einsum('bhntd,bhnvd->bhntv', q, c_in) + jnp.einsum('bhnts,bhnsv->bhntv', pm, v)
    den = w_inter * jnp.einsum('bhntd,bhnd->bhnt', q, n_in) + jnp.sum(pm, axis=-1)
    h = num / jnp.maximum(jnp.abs(den), jnp.exp(-m_t))[..., None]
    return h.reshape(B, H, L, dv), final


def _gdn_mixer(hc, hx, w_in, conv_w, a_log, dt_bias, norm_g, w_out, ctx_out):
    H, dk, dv = GDN_HEADS, GDN_DK, GDN_DV
    n_qk, n_v = H * dk, H * dv
    o_z = 2 * n_qk + n_v
    o_a = o_z + n_v
    a_decay = jnp.exp(a_log.astype(jnp.float32))
    dtb = dt_bias.astype(jnp.float32)

    def project(h):
        b, l, _ = h.shape
        p = h @ w_in
        qkv = jax.nn.silu(_centred_dwconv(p[..., :o_z], conv_w))
        q = _l2norm(_heads(qkv[..., :n_qk], H))
        k = _l2norm(_heads(qkv[..., n_qk:2 * n_qk], H))
        v = _heads(qkv[..., 2 * n_qk:], H)
        z = p[..., o_z:o_a].reshape(b, l, H, dv)
        ga = p[..., o_a:o_a + 2 * H].reshape(b, l, 2, H).astype(jnp.float32)
        gb = p[..., o_a + 2 * H:].reshape(b, l, 2, H).astype(jnp.float32)
        g = (-a_decay * jax.nn.softplus(ga + dtb)).transpose(2, 0, 3, 1)
        beta = jax.nn.sigmoid(gb).transpose(2, 0, 3, 1)
        return q, k, v, z, beta, g

    qc, kc, vc, zc, bc, gc = project(hc)
    qx, kx, vx, zx, bx, gx = project(hx)
    s0 = jnp.zeros((hx.shape[0], H, dk, dv), jnp.float32)
    oc, ox = _bidirectional_prefix_scan(_gated_delta_chunked,
                                        (qc, kc, vc, bc[0], gc[0]), (qc, kc, vc, bc[1], gc[1]),
                                        (qx, kx, vx, bx[0], gx[0]), (qx, kx, vx, bx[1], gx[1]), s0)

    def out(o, z, dtype):
        b, _, l, _ = o.shape
        o = _rmsnorm(o.transpose(0, 2, 1, 3).astype(dtype), norm_g) * jax.nn.silu(z)
        return o.reshape(b, l, H * dv) @ w_out

    yx = out(ox, zx, hx.dtype)
    yc = out(oc, zc, hc.dtype) if ctx_out else None
    return yc, yx


def _mlstm_mixer(hc, hx, w_in, b_i, b_f, norm_g, w_out, ctx_out):
    H, dqk, dv = MLSTM_HEADS, MLSTM_DQK, MLSTM_DV
    o1 = H * dqk
    o2 = 2 * o1
    o3 = o2 + H * dv
    o4 = o3 + H * dv
    bi = b_i.astype(jnp.float32)
    bf = b_f.astype(jnp.float32)

    def project(h):
        b, l, _ = h.shape
        p = h @ w_in
        q = _heads(p[..., :o1], H)
        k = _heads(p[..., o1:o2], H)
        v = _heads(p[..., o2:o3], H)
        og = p[..., o3:o4].reshape(b, l, H, dv)
        gi = p[..., o4:o4 + 2 * H].reshape(b, l, 2, H).astype(jnp.float32) + bi
        gf = p[..., o4 + 2 * H:].reshape(b, l, 2, H).astype(jnp.float32) + bf
        logi = gi.transpose(2, 0, 3, 1)
        logf = jax.nn.log_sigmoid(gf).transpose(2, 0, 3, 1)
        return q, k, v, og, logi, logf

    qc, kc, vc, oc_g, ic, fc = project(hc)
    qx, kx, vx, ox_g, ix, fx = project(hx)
    bsz = hx.shape[0]
    st0 = (jnp.zeros((bsz, H, dv, dqk), jnp.float32), jnp.zeros((bsz, H, dqk), jnp.float32),
           jnp.zeros((bsz, H), jnp.float32))
    hc_s, hx_s = _bidirectional_prefix_scan(_mlstm_chunked,
                                            (qc, kc, vc, ic[0], fc[0]), (qc, kc, vc, ic[1], fc[1]),
                                            (qx, kx, vx, ix[0], fx[0]), (qx, kx, vx, ix[1], fx[1]), st0)

    def out(hs, og, dtype):
        b, _, l, _ = hs.shape
        hs = _rmsnorm(hs.transpose(0, 2, 1, 3).astype(dtype), norm_g) * jax.nn.sigmoid(og)
        return hs.reshape(b, l, H * dv) @ w_out

    yx = out(hx_s, ox_g, hx.dtype)
    yc = out(hc_s, oc_g, hc.dtype) if ctx_out else None
    return yc, yx


def _diff_mixer(hc, hx, w_in, lam_p, norm_g, w_out, rope, lambda_init, ctx_out):
    H, dh, QB = DIFF_HEADS, DIFF_DH, ATTN_QBLOCK
    scale = dh ** -0.5
    lp = lam_p.astype(jnp.float32)
    lam = jnp.exp(jnp.sum(lp[0] * lp[1])) - jnp.exp(jnp.sum(lp[2] * lp[3])) + lambda_init

    def project(h):
        b, l, _ = h.shape
        p = h @ w_in
        q = p[..., :2 * H * dh].reshape(b, l, 2 * H, dh).transpose(0, 2, 1, 3)
        k = p[..., 2 * H * dh:4 * H * dh].reshape(b, l, 2 * H, dh).transpose(0, 2, 1, 3)
        v = p[..., 4 * H * dh:].reshape(b, l, H, 2 * dh).transpose(0, 2, 1, 3)
        return q, k, v

    def attend(q, k, v):
        s = jnp.einsum('bhqd,bhkd->bhqk', q, k).astype(jnp.float32) * scale
        p = jax.nn.softmax(s, axis=-1)
        p = p.reshape(p.shape[0], H, 2, p.shape[2], p.shape[3])
        a = p[:, :, 0] - lam * p[:, :, 1]
        return jnp.einsum('bhqk,bhkv->bhqv', a.astype(v.dtype), v)

    qc, kc, vc = project(hc)
    qx, kx, vx = project(hx)
    qx = _apply_axial_rope(qx, rope)
    kx = _apply_axial_rope(kx, rope)
    B, _, L, _ = qx.shape
    nb = L // QB
    k_all = jnp.concatenate([kc, kx], axis=2)
    v_all = jnp.concatenate([vc, vx], axis=2)
    q_blocks = qx.reshape(B, 2 * H, nb, QB, dh).transpose(2, 0, 1, 3, 4)
    ox = lax.map(lambda qb: attend(qb, k_all, v_all), q_blocks)
    ox = ox.transpose(1, 0, 3, 2, 4).reshape(B, L, H, 2 * dh)

    def out(o):
        b, l = o.shape[0], o.shape[1]
        o = _rmsnorm(o, norm_g) * (1.0 - lambda_init)
        return o.reshape(b, l, H * 2 * dh) @ w_out

    yx = out(ox)
    yc = out(attend(qc, kc, vc).transpose(0, 2, 1, 3)) if ctx_out else None
    return yc, yx


def _sink_softmax(s, sink_b):
    m = jnp.maximum(jnp.max(s, axis=-1, keepdims=True), sink_b)
    e = jnp.exp(s - m)
    return e / (jnp.sum(e, axis=-1, keepdims=True) + jnp.exp(sink_b - m))


def _swa_mixer(hc, hx, w_in, sink, w_out, rope, ctx_out):
    Hq, Hk, dh, W, QB = SWA_QHEADS, SWA_KVHEADS, SWA_DH, SWA_WINDOW, ATTN_QBLOCK
    G = Hq // Hk
    scale = dh ** -0.5
    sink_b = sink.astype(jnp.float32).reshape(1, Hk, G, 1, 1)

    def project(h):
        b, l, _ = h.shape
        p = h @ w_in
        q = p[..., :Hq * dh].reshape(b, l, Hq, dh).transpose(0, 2, 1, 3)
        k = p[..., Hq * dh:(Hq + Hk) * dh].reshape(b, l, Hk, dh).transpose(0, 2, 1, 3)
        v = p[..., (Hq + Hk) * dh:].reshape(b, l, Hk, dh).transpose(0, 2, 1, 3)
        return q, k, v

    qc, kc, vc = project(hc)
    qx, kx, vx = project(hx)
    qx = _apply_axial_rope(qx, rope)
    kx = _apply_axial_rope(kx, rope)
    B, _, L, _ = qx.shape
    Lc = kc.shape[2]
    qg = qx.reshape(B, Hk, G, L, dh)
    kp = jnp.pad(kx, ((0, 0), (0, 0), (W, W), (0, 0)))
    vp = jnp.pad(vx, ((0, 0), (0, 0), (W, W), (0, 0)))
    span = QB + 2 * W

    def block(jb):
        start = jb * QB
        qb = lax.dynamic_slice_in_dim(qg, start, QB, axis=3)
        kb = lax.dynamic_slice_in_dim(kp, start, span, axis=2)
        vb = lax.dynamic_slice_in_dim(vp, start, span, axis=2)
        q_pos = start + jnp.arange(QB)
        k_pos = start - W + jnp.arange(span)
        valid = (jnp.abs(q_pos[:, None] - k_pos[None, :]) <= W) & (k_pos >= 0) & (k_pos < L)
        s_loc = jnp.einsum('bkgqd,bksd->bkgqs', qb, kb).astype(jnp.float32) * scale
        s_loc = jnp.where(valid, s_loc, -jnp.inf)
        s_ctx = jnp.einsum('bkgqd,bksd->bkgqs', qb, kc).astype(jnp.float32) * scale
        probs = _sink_softmax(jnp.concatenate([s_ctx, s_loc], axis=-1), sink_b)
        return jnp.einsum('bkgqs,bksd->bkgqd', probs.astype(vb.dtype), jnp.concatenate([vc, vb], axis=2))

    ox = lax.map(block, jnp.arange(L // QB))
    yx = ox.transpose(1, 0, 4, 2, 3, 5).reshape(B, L, Hq * dh) @ w_out
    yc = None
    if ctx_out:
        qcg = qc.reshape(B, Hk, G, Lc, dh)
        s = jnp.einsum('bkgqd,bksd->bkgqs', qcg, kc).astype(jnp.float32) * scale
        oc = jnp.einsum('bkgqs,bksd->bkgqd', _sink_softmax(s, sink_b).astype(vc.dtype), vc)
        yc = oc.transpose(0, 3, 1, 2, 4).reshape(B, Lc, Hq * dh) @ w_out
    return yc, yx


def _moe(h, router_w, router_b, w_gu, b_gu, w_down, b_down):
    n, d = h.shape
    E, K, BLK = N_EXPERTS, TOP_K, MOE_BLOCK
    f = w_down.shape[1]
    logits = (h @ router_w).astype(jnp.float32) + router_b.astype(jnp.float32)
    top_v, top_i = lax.top_k(logits, K)
    gates = jax.nn.softmax(top_v, axis=-1)
    n_assign = n * K
    e_flat = top_i.reshape(-1)
    tok_flat = jnp.repeat(jnp.arange(n, dtype=jnp.int32), K)
    g_flat = gates.reshape(-1)
    order = jnp.argsort(e_flat)
    e_s, tok_s, g_s = e_flat[order], tok_flat[order], g_flat[order]
    counts = jnp.bincount(e_flat, length=E)
    starts = jnp.cumsum(counts) - counts
    padded = (counts + BLK - 1) // BLK * BLK
    pad_end = jnp.cumsum(padded)
    pad_start = pad_end - padded
    dest = pad_start[e_s] + (jnp.arange(n_assign) - starts[e_s])
    n_blk = (n_assign + BLK - 1) // BLK + E
    buf_tok = jnp.full((n_blk * BLK,), n, jnp.int32).at[dest].set(tok_s)
    buf_gate = jnp.zeros((n_blk * BLK,), jnp.float32).at[dest].set(g_s)
    blk_exp = jnp.minimum(jnp.searchsorted(pad_end, jnp.arange(n_blk) * BLK, side='right'), E - 1)
    h_pad = jnp.concatenate([h, jnp.zeros((1, d), h.dtype)], axis=0)

    def expert_block(args):
        toks, e = args
        xb = h_pad[toks]
        gu = (xb @ w_gu[e]).astype(jnp.float32) + b_gu[e].astype(jnp.float32)
        x_glu = jnp.minimum(gu[:, :f], SWIGLU_LIMIT)
        x_lin = jnp.clip(gu[:, f:], -SWIGLU_LIMIT, SWIGLU_LIMIT)
        act = (x_glu * jax.nn.sigmoid(SWIGLU_ALPHA * x_glu) * (x_lin + 1.0)).astype(h.dtype)
        return act @ w_down[e] + b_down[e]

    y = lax.map(expert_block, (buf_tok.reshape(n_blk, BLK), blk_exp))
    y = (y.reshape(-1, d).astype(jnp.float32) * buf_gate[:, None]).astype(h.dtype)
    return jnp.zeros((n + 1, d), h.dtype).at[buf_tok].add(y)[:n]


def setup_inputs(seed: int = 0) -> dict:
    key = jax.random.key(seed)
    ks = iter(jax.random.split(key, 48))
    D = D_MODEL

    def nrm(shape, scale):
        return jax.random.normal(next(ks), shape, jnp.float32) * scale

    nA, nB, nC, nD = [_layers_using(m) for m in range(N_MIXERS)]
    x = nrm((BATCH, SEQ, D), 1.0)
    c = nrm((BATCH, D), 1.0)
    ctx = nrm((BATCH, CTX_LEN, D), 1.0)
    c_ctx = nrm((D,), 1.0)
    ada_w = nrm((DEPTH, D, 6 * D), 0.5 * D ** -0.5)
    ada_b = nrm((DEPTH, 6 * D), 0.02)
    norm_mix_g = 1.0 + nrm((DEPTH, D), 0.02)
    norm_ffn_g = 1.0 + nrm((DEPTH, D), 0.02)
    final_g = 1.0 + nrm((D,), 0.02)
    gdn_w_in = nrm((nA, D, GDN_PROJ), D ** -0.5)
    gdn_conv_w = nrm((nA, GDN_CONV, 2 * GDN_HEADS * GDN_DK + GDN_HEADS * GDN_DV), GDN_CONV ** -0.5)
    gdn_a_log = jnp.log(jax.random.uniform(next(ks), (nA, 2, GDN_HEADS), jnp.float32, 1.0, 16.0))
    dt = jnp.exp(jax.random.uniform(next(ks), (nA, 2, GDN_HEADS), jnp.float32, math.log(1e-3), math.log(1e-1)))
    gdn_dt_bias = dt + jnp.log(-jnp.expm1(-dt))
    gdn_norm_g = 1.0 + nrm((nA, GDN_DV), 0.02)
    gdn_w_out = nrm((nA, GDN_HEADS * GDN_DV, D), (GDN_HEADS * GDN_DV) ** -0.5)
    mlstm_w_in = nrm((nB, D, MLSTM_PROJ), D ** -0.5)
    mlstm_b_i = nrm((nB, 2, MLSTM_HEADS), 0.1)
    mlstm_b_f = jnp.linspace(3.0, 6.0, MLSTM_HEADS, dtype=jnp.float32)[None, None, :] + nrm((nB, 2, MLSTM_HEADS), 0.1)
    mlstm_norm_g = 1.0 + nrm((nB, MLSTM_DV), 0.02)
    mlstm_w_out = nrm((nB, MLSTM_HEADS * MLSTM_DV, D), (MLSTM_HEADS * MLSTM_DV) ** -0.5)
    diff_w_in = nrm((nC, D, DIFF_PROJ), D ** -0.5)
    diff_lambda = nrm((nC, 4, DIFF_DH), 0.1)
    diff_norm_g = 1.0 + nrm((nC, 2 * DIFF_DH), 0.02)
    diff_w_out = nrm((nC, 2 * DIFF_HEADS * DIFF_DH, D), (2 * DIFF_HEADS * DIFF_DH) ** -0.5)
    swa_w_in = nrm((nD, D, SWA_PROJ), D ** -0.5)
    swa_sink = nrm((nD, SWA_QHEADS), 0.5)
    swa_w_out = nrm((nD, SWA_QHEADS * SWA_DH, D), (SWA_QHEADS * SWA_DH) ** -0.5)
    router_w = nrm((DEPTH, D, N_EXPERTS), D ** -0.5)
    router_b = nrm((DEPTH, N_EXPERTS), 0.01)
    moe_w_gu = nrm((DEPTH, N_EXPERTS, D, 2 * D_FF), D ** -0.5)
    moe_b_gu = nrm((DEPTH, N_EXPERTS, 2 * D_FF), 0.01)
    moe_w_down = nrm((DEPTH, N_EXPERTS, D_FF, D), D_FF ** -0.5)
    moe_b_down = nrm((DEPTH, N_EXPERTS, D), 0.01)
    return {'x': x, 'c': c, 'ctx': ctx, 'c_ctx': c_ctx, 'ada_w': ada_w, 'ada_b': ada_b,
            'norm_mix_g': norm_mix_g, 'norm_ffn_g': norm_ffn_g, 'final_g': final_g,
            'gdn_w_in': gdn_w_in, 'gdn_conv_w': gdn_conv_w, 'gdn_a_log': gdn_a_log, 'gdn_dt_bias': gdn_dt_bias,
            'gdn_norm_g': gdn_norm_g, 'gdn_w_out': gdn_w_out,
            'mlstm_w_in': mlstm_w_in, 'mlstm_b_i': mlstm_b_i, 'mlstm_b_f': mlstm_b_f,
            'mlstm_norm_g': mlstm_norm_g, 'mlstm_w_out': mlstm_w_out,
            'diff_w_in': diff_w_in, 'diff_lambda': diff_lambda, 'diff_norm_g': diff_norm_g, 'diff_w_out': diff_w_out,
            'swa_w_in': swa_w_in, 'swa_sink': swa_sink, 'swa_w_out': swa_w_out,
            'router_w': router_w, 'router_b': router_b, 'moe_w_gu': moe_w_gu, 'moe_b_gu': moe_b_gu,
            'moe_w_down': moe_w_down, 'moe_b_down': moe_b_down}


def reference(x, c, ctx, c_ctx, ada_w, ada_b, norm_mix_g, norm_ffn_g, final_g,
              gdn_w_in, gdn_conv_w, gdn_a_log, gdn_dt_bias, gdn_norm_g, gdn_w_out,
              mlstm_w_in, mlstm_b_i, mlstm_b_f, mlstm_norm_g, mlstm_w_out,
              diff_w_in, diff_lambda, diff_norm_g, diff_w_out,
              swa_w_in, swa_sink, swa_w_out,
              router_w, router_b, moe_w_gu, moe_b_gu, moe_w_down, moe_b_down):
    B, L, D = x.shape
    Lc = ctx.shape[1]
    rope = _axial_rope_tables(L, DIFF_DH)
    silu_c = jax.nn.silu(c)
    silu_cc = jax.nn.silu(c_ctx)
    xc = ctx
    for i in range(DEPTH):
        last = i == DEPTH - 1
        kind, j = i % N_MIXERS, i // N_MIXERS
        mx = jnp.split(silu_c @ ada_w[i] + ada_b[i], 6, axis=-1)
        sh1x, sc1x, g1x, sh2x, sc2x, g2x = [m[:, None, :] for m in mx]
        sh1c, sc1c, g1c, sh2c, sc2c, g2c = jnp.split(silu_cc @ ada_w[i] + ada_b[i], 6, axis=-1)
        hx = _modulate(_rmsnorm(x, norm_mix_g[i]), sh1x, sc1x)
        hc = _modulate(_rmsnorm(xc, norm_mix_g[i]), sh1c, sc1c)
        if kind == 0:
            yc, yx = _gdn_mixer(hc, hx, gdn_w_in[j], gdn_conv_w[j], gdn_a_log[j], gdn_dt_bias[j],
                                gdn_norm_g[j], gdn_w_out[j], not last)
        elif kind == 1:
            yc, yx = _mlstm_mixer(hc, hx, mlstm_w_in[j], mlstm_b_i[j], mlstm_b_f[j], mlstm_norm_g[j],
                                  mlstm_w_out[j], not last)
        elif kind == 2:
            lambda_init = 0.8 - 0.6 * math.exp(-0.3 * i)
            yc, yx = _diff_mixer(hc, hx, diff_w_in[j], diff_lambda[j], diff_norm_g[j], diff_w_out[j],
                                 rope, lambda_init, not last)
        else:
            yc, yx = _swa_mixer(hc, hx, swa_w_in[j], swa_sink[j], swa_w_out[j], rope, not last)
        x = x + g1x * yx
        fx = _modulate(_rmsnorm(x, norm_ffn_g[i]), sh2x, sc2x).reshape(B * L, D)
        if last:
            x = x + g2x * _moe(fx, router_w[i], router_b[i], moe_w_gu[i], moe_b_gu[i],
                               moe_w_down[i], moe_b_down[i]).reshape(B, L, D)
        else:
            xc = xc + g1c * yc
            fc = _modulate(_rmsnorm(xc, norm_ffn_g[i]), sh2c, sc2c).reshape(B * Lc, D)
            fo = _moe(jnp.concatenate([fx, fc], axis=0), router_w[i], router_b[i], moe_w_gu[i], moe_b_gu[i],
                      moe_w_down[i], moe_b_down[i])
            x = x + g2x * fo[:B * L].reshape(B, L, D)
            xc = xc + g2c * fo[B * L:].reshape(B, Lc, D)
    return _rmsnorm(x, final_g)
```

```python
import functools
import math

import jax
import jax.numpy as jnp
from jax import lax
from jax.experimental import pallas as pl
from jax.experimental.pallas import tpu as pltpu

F32 = jnp.float32
BF16 = jnp.bfloat16
HI = lax.Precision.HIGHEST

NORM_EPS = 1e-6
ROPE_BASE = 10000.0
GRID_W = 64
TM = 256
CHUNK = 64
LANES = 128
N_EXPERTS = 32
TOP_K = 4
SWIGLU_ALPHA = 1.702
SWIGLU_LIMIT = 7.0
MOE_BM = 256
SWA_WINDOW = 128
VMEM_LIMIT = 56 * 1024 * 1024


def _cp(sem, vmem=VMEM_LIMIT):
    return pltpu.CompilerParams(dimension_semantics=sem, vmem_limit_bytes=vmem)


def _dot(a, b):
    return jnp.dot(a.astype(BF16), b.astype(BF16), preferred_element_type=F32)


def _dot_nt(a, b):
    return lax.dot_general(a.astype(BF16), b.astype(BF16), (((1,), (1,)), ((), ())),
                           preferred_element_type=F32)


def _dot_tn(a, b):
    return lax.dot_general(a.astype(BF16), b.astype(BF16), (((0,), (0,)), ((), ())),
                           preferred_element_type=F32)


def _dot_hi(a, b):
    return jnp.dot(a, b, precision=HI, preferred_element_type=F32)


def _sigmoid(x):
    return 1.0 / (1.0 + jnp.exp(-x))


def _softplus(x):
    return jnp.maximum(x, 0.0) + jnp.log(1.0 + jnp.exp(-jnp.abs(x)))


def _mods_kernel(c_ref, w_ref, b_ref, o_ref):
    c = c_ref[...]
    o_ref[...] = _dot_hi(c * _sigmoid(c), w_ref[...]) + b_ref[...]


def _ada_mods(cond8, ada_w, ada_b):
    depth, d, d6 = ada_w.shape
    return pl.pallas_call(
        _mods_kernel,
        out_shape=jax.ShapeDtypeStruct((depth, 8, d6), F32),
        grid=(depth, d6 // d),
        in_specs=[pl.BlockSpec((8, d), lambda i, j: (0, 0)),
                  pl.BlockSpec((None, d, d), lambda i, j: (i, 0, j)),
                  pl.BlockSpec((None, 1, d), lambda i, j: (i, 0, j))],
        out_specs=pl.BlockSpec((None, 8, d), lambda i, j: (i, 0, j)),
        compiler_params=_cp(("arbitrary", "arbitrary")),
        name="ada_mods",
    )(cond8, ada_w, ada_b.reshape(depth, 1, d6))


def _mod_row(m_ref, row, k, d):
    return m_ref[pl.ds(row, 1), k * d:(k + 1) * d]


def _prenorm_mod(x, g, shift, scale):
    ms = jnp.mean(x * x, axis=-1, keepdims=True)
    y = x * lax.rsqrt(ms + NORM_EPS) * g
    return y * (1.0 + scale) + shift


def _rope_slab(x, cos, sin, first_half):
    part = jnp.where(first_half, pltpu.roll(x, LANES - 16, 1), pltpu.roll(x, 16, 1))
    return x * cos + part * sin


def _inproj_kernel(kind, nct, nb, d, *refs):
    b = pl.program_id(0)
    t = pl.program_id(1)
    row = jnp.where(t < nct, nb, b)
    x_ref, g_ref, m_ref, w_ref = refs[:4]
    rest = refs[4:]
    h = _prenorm_mod(x_ref[...], g_ref[...], _mod_row(m_ref, row, 0, d), _mod_row(m_ref, row, 1, d))
    p = _dot(h, w_ref[...])
    if kind in (0, 1):
        wg_ref, c0_ref, c1_ref, p_ref, gt_ref, gc_ref = rest
        p_ref[...] = p
        graw = _dot_hi(h, wg_ref[...])
        lane = lax.broadcasted_iota(jnp.int32, graw.shape, 1)
        if kind == 0:
            gate = jnp.where(lane < 16, -jnp.exp(c0_ref[...]) * _softplus(graw + c1_ref[...]),
                             _sigmoid(graw))
        else:
            gate = jnp.where(lane < 16, graw + c0_ref[...], -_softplus(-(graw + c1_ref[...])))
        gt_ref[...] = gate
        gc_ref[...] = gate.T
    else:
        cos_ref, sin_ref, q_ref, k_ref, v_ref = rest
        cos = cos_ref[...]
        sin = sin_ref[...]
        lane = lax.broadcasted_iota(jnp.int32, cos.shape, 1)
        first_half = (lane % 32) < 16
        nq = q_ref.shape[-1] // LANES
        nk = k_ref.shape[-1] // LANES
        for s in range(nq):
            q_ref[:, s * LANES:(s + 1) * LANES] = (
                _rope_slab(p[:, s * LANES:(s + 1) * LANES], cos, sin, first_half) * 0.125).astype(BF16)
        for s in range(nk):
            o = (nq + s) * LANES
            k_ref[:, s * LANES:(s + 1) * LANES] = _rope_slab(p[:, o:o + LANES], cos, sin, first_half).astype(BF16)
        o = (nq + nk) * LANES
        v_ref[...] = p[:, o:].astype(BF16)


def _inproj(kind, xt, g, mods, w_main, extra, nct):
    nb, t_all, d = xt.shape
    nt = t_all // TM
    pdim = w_main.shape[1]
    tile = lambda w: pl.BlockSpec((None, TM, w), lambda b, t: (b, t, 0))
    full = lambda a: pl.BlockSpec(a.shape, lambda b, t: (0,) * a.ndim)
    in_specs = [tile(d), full(g), full(mods), full(w_main)]
    if kind in (0, 1):
        wg, c0, c1 = extra
        in_specs += [full(wg), full(c0), full(c1)]
        out_shape = (jax.ShapeDtypeStruct((nb, t_all, pdim), F32),
                     jax.ShapeDtypeStruct((nb, t_all, LANES), F32),
                     jax.ShapeDtypeStruct((nb, LANES, t_all), F32))
        out_specs = (tile(pdim), tile(LANES), pl.BlockSpec((None, LANES, TM), lambda b, t: (b, 0, t)))
        args = (xt, g, mods, w_main, wg, c0, c1)
    else:
        cos, sin, widths = extra
        in_specs += [pl.BlockSpec((TM, LANES), lambda b, t: (t, 0))] * 2
        out_shape = tuple(jax.ShapeDtypeStruct((nb, t_all, w), BF16) for w in widths)
        out_specs = tuple(tile(w) for w in widths)
        args = (xt, g, mods, w_main, cos, sin)
    return pl.pallas_call(
        functools.partial(_inproj_kernel, kind, nct, nb, d),
        out_shape=out_shape, grid=(nb, nt), in_specs=in_specs, out_specs=out_specs,
        compiler_params=_cp(("arbitrary", "arbitrary")), name=f"inproj{kind}",
    )(*args)


def _gdn_prep_kernel(nct, nt, p_ref, prev_ref, next_ref, w_ref, q_ref, k_ref, v_ref, win_ref):
    t = pl.program_id(1)
    prev_ok = jnp.logical_and(t != 0, t != nct)
    next_ok = jnp.logical_and(t != nct - 1, t != nt - 1)
    win_ref[0:8, :] = jnp.where(prev_ok, prev_ref[...], 0.0)
    win_ref[8:8 + TM, :] = p_ref[...]
    win_ref[8 + TM:, :] = jnp.where(next_ok, next_ref[...], 0.0)
    width = q_ref.shape[-1]
    for part, o_ref in enumerate((q_ref, k_ref, v_ref)):
        for h in range(width // LANES):
            c0 = part * width + h * LANES
            acc = None
            for j in range(5):
                term = win_ref[6 + j:6 + j + TM, c0:c0 + LANES] * w_ref[j:j + 1, c0:c0 + LANES]
                acc = term if acc is None else acc + term
            y = acc * _sigmoid(acc)
            if part < 2:
                y = y * lax.rsqrt(jnp.sum(y * y, axis=-1, keepdims=True) + NORM_EPS)
            if part == 0:
                y = y * (LANES ** -0.5)
            o_ref[:, h * LANES:(h + 1) * LANES] = y


def _gdn_prep(p_main, conv_w8, nct):
    nb, t_all, _ = p_main.shape
    nt = t_all // TM
    width = conv_w8.shape[1] // 3
    cw = 3 * width
    t8 = t_all // 8
    r8 = TM // 8
    tile = pl.BlockSpec((None, TM, width), lambda b, t: (b, t, 0))
    return pl.pallas_call(
        functools.partial(_gdn_prep_kernel, nct, nt),
        out_shape=tuple(jax.ShapeDtypeStruct((nb, t_all, width), F32) for _ in range(3)),
        grid=(nb, nt),
        in_specs=[pl.BlockSpec((None, TM, cw), lambda b, t: (b, t, 0)),
                  pl.BlockSpec((None, 8, cw), lambda b, t: (b, jnp.maximum(t * r8 - 1, 0), 0)),
                  pl.BlockSpec((None, 8, cw), lambda b, t: (b, jnp.minimum((t + 1) * r8, t8 - 1), 0)),
                  pl.BlockSpec(conv_w8.shape, lambda b, t: (0, 0))],
        out_specs=(tile, tile, tile),
        scratch_shapes=[pltpu.VMEM((TM + 16, cw), F32)],
        compiler_params=_cp(("arbitrary", "arbitrary")), name="gdn_prep",
    )(p_main, p_main, p_main, conv_w8)


def _chunk_masks(rev):
    ii = lax.broadcasted_iota(jnp.int32, (CHUNK, CHUNK), 0)
    jj = lax.broadcasted_iota(jnp.int32, (CHUNK, CHUNK), 1)
    if rev:
        return ii <= jj, ii < jj, ii == jj
    return ii >= jj, ii > jj, ii == jj


def _cumsum_forms(col, row, rev):
    incl, _, _ = _chunk_masks(rev)
    incl_t, _, _ = _chunk_masks(not rev)
    cmat = _dot_hi(jnp.where(incl, 1.0, 0.0), jnp.broadcast_to(col, (CHUNK, CHUNK)))
    crow = _dot_hi(jnp.broadcast_to(row, (8, CHUNK)), jnp.where(incl_t, 1.0, 0.0))[0:1]
    return cmat, crow


def _gdn_chunk(q, k, v, beta, gcol, grow, s, rev):
    incl, strict, eye = _chunk_masks(rev)
    gmat, g_row = _cumsum_forms(gcol, grow, rev)
    g_col = gmat[:, 0:1]
    gam = jnp.where(incl, jnp.exp(jnp.where(incl, gmat - g_row, 0.0)), 0.0)
    kb = k * beta
    a = jnp.where(strict, _dot_nt(kb, k) * gam, 0.0)
    tm = jnp.where(eye, 1.0, 0.0) - a
    pw = _dot_hi(a, a)
    for m in range(5):
        tm = tm + _dot_hi(tm, pw)
        if m < 4:
            pw = _dot_hi(pw, pw)
    eg = jnp.exp(g_col)
    u = _dot_hi(tm, v * beta)
    w = _dot_hi(tm, kb * eg)
    qk = _dot_nt(q, k) * gam
    gtot = jnp.sum(gcol, axis=0, keepdims=True)
    kd = k * jnp.exp(gtot - g_col)
    v_new = u - _dot(w, s)
    o = _dot(q * eg, s) + _dot(qk, v_new)
    s = s * jnp.exp(gtot) + _dot_tn(kd, v_new)
    return o, s


def _gate_col(gt, idx):
    lane = lax.broadcasted_iota(jnp.int32, gt.shape, 1)
    return jnp.sum(jnp.where(lane == idx, gt, 0.0), axis=1, keepdims=True)


def _gate_row(gc, idx):
    sub = lax.broadcasted_iota(jnp.int32, gc.shape, 0)
    return jnp.sum(jnp.where(sub == idx, gc, 0.0), axis=0, keepdims=True)


def _gdn_scan_kernel(qf, kf, vf, gtf, gcf, qb, kb, vb, gtb, gcb, of_ref, ob_ref, sf_ref, sb_ref):
    h = pl.program_id(1)

    @pl.when(pl.program_id(2) == 0)
    def _():
        sf_ref[...] = jnp.zeros_like(sf_ref)
        sb_ref[...] = jnp.zeros_like(sb_ref)

    nchunk = TM // CHUNK
    for rev, (q_r, k_r, v_r, gt_r, gc_r, o_r, s_r) in enumerate(
            ((qf, kf, vf, gtf, gcf, of_ref, sf_ref), (qb, kb, vb, gtb, gcb, ob_ref, sb_ref))):
        gt = gt_r[...]
        gc = gc_r[...]
        gcol = _gate_col(gt, rev * 8 + h)
        bcol = _gate_col(gt, 16 + rev * 8 + h)
        grow = _gate_row(gc, rev * 8 + h)
        s = s_r[...]
        order = range(nchunk - 1, -1, -1) if rev else range(nchunk)
        for c in order:
            sl = slice(c * CHUNK, (c + 1) * CHUNK)
            o, s = _gdn_chunk(q_r[sl, :], k_r[sl, :], v_r[sl, :], bcol[sl, :], gcol[sl, :], grow[:, sl], s, bool(rev))
            o_r[sl, :] = o
        s_r[...] = s


def _scan_tiles(nct, nt):
    fwd = lambda s: s
    bwd = lambda s: jnp.where(s < nct, nct - 1 - s, nt - 1 - (s - nct))
    return fwd, bwd


def _gdn_scan(q, k, v, gt, gc, nct):
    nb, t_all, width = q.shape
    nh = width // LANES
    nt = t_all // TM
    fwd, bwd = _scan_tiles(nct, nt)

    def specs(tile_of):
        head = pl.BlockSpec((None, TM, LANES), lambda b, h, s: (b, tile_of(s), h))
        return [head, head, head,
                pl.BlockSpec((None, TM, LANES), lambda b, h, s: (b, tile_of(s), 0)),
                pl.BlockSpec((None, LANES, TM), lambda b, h, s: (b, 0, tile_of(s)))]

    out = jax.ShapeDtypeStruct((nb, t_all, width), F32)
    return pl.pallas_call(
        _gdn_scan_kernel, out_shape=(out, out), grid=(nb, nh, nt),
        in_specs=specs(fwd) + specs(bwd),
        out_specs=(pl.BlockSpec((None, TM, LANES), lambda b, h, s: (b, fwd(s), h)),
                   pl.BlockSpec((None, TM, LANES), lambda b, h, s: (b, bwd(s), h))),
        scratch_shapes=[pltpu.VMEM((LANES, LANES), F32), pltpu.VMEM((LANES, LANES), F32)],
        compiler_params=_cp(("arbitrary", "arbitrary", "arbitrary")), name="gdn_scan",
    )(q, k, v, gt, gc, q, k, v, gt, gc)


def _mlstm_chunk(q, k, v_aug, li_col, li_row, lf_col, lf_row, c_in, m_in, rev):
    incl, _, _ = _chunk_masks(rev)
    bmat, b_row = _cumsum_forms(lf_col, lf_row, rev)
    b_col = bmat[:, 0:1]
    dlog = jnp.where(incl, bmat - b_row + li_row, -jnp.inf)
    m_intra = jnp.max(dlog, axis=-1, keepdims=True)
    btot = jnp.sum(lf_col, axis=0, keepdims=True)
    a_loc = btot - b_col + li_col
    m_loc = jnp.max(a_loc, axis=0, keepdims=True)
    c_loc = _dot_tn(k * jnp.exp(a_loc - m_loc), v_aug)
    log_inter = b_col + m_in
    m_t = jnp.maximum(log_inter, m_intra)
    w_inter = jnp.exp(log_inter - m_t)
    pm = jnp.exp(dlog - m_t) * _dot_nt(q, k)
    num = w_inter * _dot(q, c_in) + _dot(pm, v_aug)
    dv = v_aug.shape[1] // 2
    den = num[:, dv:dv + 1]
    hout = num[:, :dv] / jnp.maximum(jnp.abs(den), jnp.exp(-m_t))
    m_new = jnp.maximum(btot + m_in, m_loc)
    c_new = c_in * jnp.exp(btot + m_in - m_new) + c_loc * jnp.exp(m_loc - m_new)
    return hout, c_new, m_new


def _mlstm_scan_kernel(qf, kf, vf, gtf, gcf, qb, kb, vb, gtb, gcb, of_ref, ob_ref, c_ref, m_ref):
    hp = pl.program_id(1)

    @pl.when(pl.program_id(2) == 0)
    def _():
        c_ref[...] = jnp.zeros_like(c_ref)
        m_ref[...] = jnp.zeros_like(m_ref)

    nchunk = TM // CHUNK
    dqk = qf.shape[-1] // 2
    dv = vf.shape[-1] // 2
    lane = lax.broadcasted_iota(jnp.int32, (TM, dv), 1)
    ones_pad = jnp.where(lane == 0, 1.0, 0.0)
    for rev, (q_r, k_r, v_r, gt_r, gc_r, o_r) in enumerate(
            ((qf, kf, vf, gtf, gcf, of_ref), (qb, kb, vb, gtb, gcb, ob_ref))):
        gt = gt_r[...]
        gc = gc_r[...]
        for j in range(2):
            head = 2 * hp + j
            li_col = _gate_col(gt, rev * 8 + head)
            lf_col = _gate_col(gt, 16 + rev * 8 + head)
            li_row = _gate_row(gc, rev * 8 + head)
            lf_row = _gate_row(gc, 16 + rev * 8 + head)
            q_all = q_r[:, j * dqk:(j + 1) * dqk]
            k_all = k_r[:, j * dqk:(j + 1) * dqk] * (dqk ** -0.5)
            v_all = jnp.concatenate([v_r[:, j * dv:(j + 1) * dv], ones_pad], axis=1)
            c_st = c_ref[rev, j]
            m_st = m_ref[rev, j][0:1, 0:1]
            order = range(nchunk - 1, -1, -1) if rev else range(nchunk)
            for c in order:
                sl = slice(c * CHUNK, (c + 1) * CHUNK)
                hout, c_st, m_st = _mlstm_chunk(q_all[sl], k_all[sl], v_all[sl], li_col[sl], li_row[:, sl],
                                                lf_col[sl], lf_row[:, sl], c_st, m_st, bool(rev))
                o_r[sl, j * dv:(j + 1) * dv] = hout
            c_ref[rev, j] = c_st
            m_ref[rev, j] = jnp.broadcast_to(m_st, (8, LANES))


def _mlstm_scan(q, k, v, gt, gc, nct):
    nb, t_all, qw = q.shape
    vw = v.shape[-1]
    nhp = qw // LANES
    nt = t_all // TM
    dqk = LANES // 2
    dv = vw // nhp // 2
    fwd, bwd = _scan_tiles(nct, nt)

    def specs(tile_of):
        qk = pl.BlockSpec((None, TM, LANES), lambda b, h, s: (b, tile_of(s), h))
        return [qk, qk,
                pl.BlockSpec((None, TM, 2 * dv), lambda b, h, s: (b, tile_of(s), h)),
                pl.BlockSpec((None, TM, LANES), lambda b, h, s: (b, tile_of(s), 0)),
                pl.BlockSpec((None, LANES, TM), lambda b, h, s: (b, 0, tile_of(s)))]

    out = jax.ShapeDtypeStruct((nb, t_all, vw), F32)
    return pl.pallas_call(
        _mlstm_scan_kernel, out_shape=(out, out), grid=(nb, nhp, nt),
        in_specs=specs(fwd) + specs(bwd),
        out_specs=(pl.BlockSpec((None, TM, 2 * dv), lambda b, h, s: (b, fwd(s), h)),
                   pl.BlockSpec((None, TM, 2 * dv), lambda b, h, s: (b, bwd(s), h))),
        scratch_shapes=[pltpu.VMEM((2, 2, dqk, 2 * dv), F32), pltpu.VMEM((2, 2, 8, LANES), F32)],
        compiler_params=_cp(("arbitrary", "arbitrary", "arbitrary")), name="mlstm_scan",
    )(q, k, v, gt, gc, q, k, v, gt, gc)


def _diff_attn_kernel(nct, nk, lambda_init, lam_ref, q_ref, k_ref, v_ref, o_ref):
    i = pl.program_id(2)
    dh = LANES // 2
    q = q_ref[...]
    n_kv = jnp.where(i < nct, nct, nk)
    lp = lam_ref[...]
    lam = (jnp.exp(jnp.sum(lp[0:1] * lp[1:2], axis=-1, keepdims=True))
           - jnp.exp(jnp.sum(lp[2:3] * lp[3:4], axis=-1, keepdims=True)) + lambda_init)

    def body(j, carry):
        off = pl.multiple_of(j * TM, TM)
        kblk = k_ref[pl.ds(off, TM), :]
        vblk = v_ref[pl.ds(off, TM), :]
        new = []
        for hh in range(2):
            m, l, acc = carry[hh]
            s = lax.dot_general(q[:, hh * dh:(hh + 1) * dh], kblk[:, hh * dh:(hh + 1) * dh],
                                (((1,), (1,)), ((), ())), preferred_element_type=F32)
            m_new = jnp.maximum(m, jnp.max(s, axis=-1, keepdims=True))
            alpha = jnp.exp(m - m_new)
            p = jnp.exp(s - m_new)
            l = alpha * l + jnp.sum(p, axis=-1, keepdims=True)
            acc = alpha * acc + jnp.dot(p.astype(BF16), vblk, preferred_element_type=F32)
            new.append((m_new, l, acc))
        return tuple(new)

    init = tuple((jnp.full((TM, 1), -jnp.inf, F32), jnp.zeros((TM, 1), F32), jnp.zeros((TM, LANES), F32))
                 for _ in range(2))
    (m1, l1, a1), (m2, l2, a2) = lax.fori_loop(0, n_kv, body, init)
    o_ref[...] = a1 / l1 - lam * (a2 / l2)


def _diff_attn(q, k, v, lam_p, lambda_init, nct):
    nb, t_all, width = q.shape
    nh = width // LANES
    nt = t_all // TM
    kv = pl.BlockSpec((None, t_all, LANES), lambda b, h, i: (b, 0, h))
    qo = pl.BlockSpec((None, TM, LANES), lambda b, h, i: (b, i, h))
    return pl.pallas_call(
        functools.partial(_diff_attn_kernel, nct, nt, lambda_init),
        out_shape=jax.ShapeDtypeStruct((nb, t_all, width), F32),
        grid=(nb, nh, nt),
        in_specs=[pl.BlockSpec(lam_p.shape, lambda b, h, i: (0, 0)), qo, kv, kv],
        out_specs=qo,
        compiler_params=_cp(("arbitrary", "arbitrary", "arbitrary")), name="diff_attn",
    )(lam_p, q, k, v)


def _swa_kernel(nct, nt, seq, sink_ref, q_ref, kc_ref, kp_ref, k0_ref, kn_ref, vc_ref, vp_ref, v0_ref, vn_ref,
                o_ref):
    i = pl.program_id(1)
    dh = LANES // 2
    nkv = kc_ref.shape[-1] // dh
    group = (q_ref.shape[-1] // dh) // nkv
    lc = kc_ref.shape[0]
    col = lax.broadcasted_iota(jnp.int32, (TM, lc + 3 * TM), 1)
    qpos = (i - nct) * TM + lax.broadcasted_iota(jnp.int32, (TM, lc + 3 * TM), 0)
    kpos = (i - nct - 1) * TM + col - lc
    ninf = -jnp.inf
    in_window = jnp.where(jnp.abs(qpos - kpos) <= SWA_WINDOW,
                          jnp.where(kpos >= 0, jnp.where(kpos < seq, 0.0, ninf), ninf), ninf)
    bias = jnp.where(col < lc, 0.0, jnp.where(i >= nct, in_window, ninf))
    kcat = jnp.concatenate([kc_ref[...], kp_ref[...], k0_ref[...], kn_ref[...]], axis=0)
    vcat = jnp.concatenate([vc_ref[...], vp_ref[...], v0_ref[...], vn_ref[...]], axis=0)
    sink = sink_ref[...]
    for g in range(nkv):
        kg = kcat[:, g * dh:(g + 1) * dh]
        vg = vcat[:, g * dh:(g + 1) * dh]
        for r in range(group):
            hq = g * group + r
            s = lax.dot_general(q_ref[:, hq * dh:(hq + 1) * dh], kg, (((1,), (1,)), ((), ())),
                                preferred_element_type=F32)
            s = s + bias
            sb = sink[0:1, hq:hq + 1]
            m = jnp.maximum(jnp.max(s, axis=-1, keepdims=True), sb)
            e = jnp.exp(s - m)
            p = e / (jnp.sum(e, axis=-1, keepdims=True) + jnp.exp(sb - m))
            o_ref[:, hq * dh:(hq + 1) * dh] = jnp.dot(p.astype(BF16), vg, preferred_element_type=F32)


def _swa_attn(q, k, v, sink_row, nct, seq):
    nb, t_all, qw = q.shape
    kw = k.shape[-1]
    nt = t_all // TM
    lc = nct * TM
    ctxs = pl.BlockSpec((None, lc, kw), lambda b, i: (b, 0, 0))
    prev = pl.BlockSpec((None, TM, kw), lambda b, i: (b, jnp.maximum(i - 1, 0), 0))
    cur = pl.BlockSpec((None, TM, kw), lambda b, i: (b, i, 0))
    nxt = pl.BlockSpec((None, TM, kw), lambda b, i: (b, jnp.minimum(i + 1, nt - 1), 0))
    qo = pl.BlockSpec((None, TM, qw), lambda b, i: (b, i, 0))
    return pl.pallas_call(
        functools.partial(_swa_kernel, nct, nt, seq),
        out_shape=jax.ShapeDtypeStruct((nb, t_all, qw), F32),
        grid=(nb, nt),
        in_specs=[pl.BlockSpec(sink_row.shape, lambda b, i: (0, 0)), qo, ctxs, prev, cur, nxt, ctxs, prev, cur, nxt],
        out_specs=qo,
        compiler_params=_cp(("arbitrary", "arbitrary")), name="swa_attn",
    )(sink_row, q, k, k, k, k, v, v, v, v)


def _headnorm(o, g_row):
    outs = []
    for h in range(o.shape[-1] // LANES):
        seg = o[:, h * LANES:(h + 1) * LANES]
        ms = jnp.mean(seg * seg, axis=-1, keepdims=True)
        outs.append(seg * lax.rsqrt(ms + NORM_EPS) * g_row)
    return jnp.concatenate(outs, axis=1)


def _outproj_kernel(kind, nct, nb, d, out_scale, *refs):
    b = pl.program_id(0)
    t = pl.program_id(1)
    row = jnp.where(t < nct, nb, b)
    if kind in (0, 1):
        of_ref, ob_ref, z_ref, hg_ref = refs[:4]
        rest = refs[4:]
        z = z_ref[...]
        o = _headnorm(of_ref[...] + ob_ref[...], hg_ref[...]) * (z * _sigmoid(z) if kind == 0 else _sigmoid(z))
    elif kind == 2:
        o_ref, hg_ref = refs[:2]
        rest = refs[2:]
        o = _headnorm(o_ref[...], hg_ref[...]) * out_scale
    else:
        o_ref = refs[0]
        rest = refs[1:]
        o = o_ref[...]
    x_ref, m_ref, w_ref, gf_ref, rw_ref, rb_ref, x1_ref, f_ref, ri_ref, cnt_ref, carry_ref = rest

    @pl.when(jnp.logical_and(b == 0, t == 0))
    def _():
        carry_ref[...] = jnp.zeros_like(carry_ref)

    x1 = x_ref[...] + _mod_row(m_ref, row, 2, d) * _dot(o, w_ref[...])
    x1_ref[...] = x1
    f = _prenorm_mod(x1, gf_ref[...], _mod_row(m_ref, row, 3, d), _mod_row(m_ref, row, 4, d))
    f_ref[...] = f
    logits = _dot_hi(f, rw_ref[...]) + rb_ref[...]
    lane = lax.broadcasted_iota(jnp.int32, logits.shape, 1)
    vals, idxs, sels = [], [], []
    cur = logits
    for _ in range(TOP_K):
        m = jnp.max(cur, axis=-1, keepdims=True)
        idx = jnp.min(jnp.where(cur == m, lane, LANES), axis=-1, keepdims=True)
        sel = lane == idx
        cur = jnp.where(sel, -jnp.inf, cur)
        vals.append(m)
        idxs.append(idx)
        sels.append(sel)
    es = [jnp.exp(v - vals[0]) for v in vals]
    esum = es[0] + es[1] + es[2] + es[3]
    assign = jnp.where(sels[0] | sels[1] | sels[2] | sels[3], 1.0, 0.0)
    ii = lax.broadcasted_iota(jnp.int32, (TM, TM), 0)
    jj = lax.broadcasted_iota(jnp.int32, (TM, TM), 1)
    before = _dot(jnp.where(ii > jj, 1.0, 0.0), assign) + carry_ref[...]
    info = jnp.zeros(logits.shape, F32)
    for k in range(TOP_K):
        rank = jnp.sum(jnp.where(sels[k], before, 0.0), axis=-1, keepdims=True)
        info = jnp.where(lane == k, idxs[k].astype(F32), info)
        info = jnp.where(lane == TOP_K + k, rank, info)
        info = jnp.where(lane == 2 * TOP_K + k, es[k] / esum, info)
    ri_ref[...] = info
    carry_ref[...] = carry_ref[...] + jnp.sum(assign, axis=0, keepdims=True)
    cnt_ref[...] = carry_ref[...]


def _outproj(kind, mixer_out, xt, mods, w_out, gf, rw, rb, nct, out_scale=1.0):
    nb, t_all, d = xt.shape
    nt = t_all // TM
    tile = lambda w: pl.BlockSpec((None, TM, w), lambda b, t: (b, t, 0))
    full = lambda a: pl.BlockSpec(a.shape, lambda b, t: (0,) * a.ndim)
    if kind in (0, 1):
        of, ob, pm, zoff, hg = mixer_out
        zblk = zoff // d
        lead = [of, ob, pm, hg]
        lead_specs = [tile(d), tile(d), pl.BlockSpec((None, TM, d), lambda b, t: (b, t, zblk)), full(hg)]
    elif kind == 2:
        o, hg = mixer_out
        lead = [o, hg]
        lead_specs = [tile(d), full(hg)]
    else:
        lead = [mixer_out]
        lead_specs = [tile(d)]
    args = lead + [xt, mods, w_out, gf, rw, rb]
    in_specs = lead_specs + [tile(d), full(mods), full(w_out), full(gf), full(rw), full(rb)]
    return pl.pallas_call(
        functools.partial(_outproj_kernel, kind, nct, nb, d, out_scale),
        out_shape=(jax.ShapeDtypeStruct((nb, t_all, d), F32), jax.ShapeDtypeStruct((nb, t_all, d), F32),
                   jax.ShapeDtypeStruct((nb, t_all, LANES), F32), jax.ShapeDtypeStruct((1, LANES), F32)),
        grid=(nb, nt), in_specs=in_specs,
        out_specs=(tile(d), tile(d), tile(LANES), pl.BlockSpec((1, LANES), lambda b, t: (0, 0))),
        scratch_shapes=[pltpu.VMEM((1, LANES), F32)],
        compiler_params=_cp(("arbitrary", "arbitrary")), name=f"outproj{kind}",
    )(*args)


def _dispatch_kernel(pos_ref, f_hbm, xs_in, xs_out, sem):
    del xs_in
    i = pl.program_id(0)

    def row_copy(r, k):
        return pltpu.make_async_copy(f_hbm.at[pl.ds(i * TM + r, 1)], xs_out.at[pl.ds(pos_ref[TOP_K * r + k], 1)], sem)

    def start(r, c):
        for k in range(TOP_K):
            row_copy(r, k).start()
        return c

    def wait(r, c):
        for k in range(TOP_K):
            row_copy(r, k).wait()
        return c

    lax.fori_loop(0, TM, start, 0)
    lax.fori_loop(0, TM, wait, 0)


def _dispatch(f2, pos_flat, n_rows):
    n, d = f2.shape
    return pl.pallas_call(
        _dispatch_kernel,
        out_shape=jax.ShapeDtypeStruct((n_rows, d), F32),
        grid=(n // TM,),
        in_specs=[pl.BlockSpec((TOP_K * TM,), lambda i: (i,), memory_space=pltpu.SMEM),
                  pl.BlockSpec(memory_space=pl.ANY), pl.BlockSpec(memory_space=pl.ANY)],
        out_specs=pl.BlockSpec(memory_space=pl.ANY),
        scratch_shapes=[pltpu.SemaphoreType.DMA],
        input_output_aliases={2: 0},
        compiler_params=_cp(("arbitrary",)), name="moe_dispatch",
    )(pos_flat, f2, jnp.zeros((n_rows, d), F32))


def _expert_kernel(be_ref, xs_ref, wgu_ref, bgu_ref, wd_ref, bd_ref, ys_ref):
    del be_ref
    ff = wd_ref.shape[0]
    gu = _dot(xs_ref[...], wgu_ref[...]) + bgu_ref[...]
    x_glu = jnp.minimum(gu[:, :ff], SWIGLU_LIMIT)
    x_lin = jnp.clip(gu[:, ff:], -SWIGLU_LIMIT, SWIGLU_LIMIT)
    act = x_glu * _sigmoid(SWIGLU_ALPHA * x_glu) * (x_lin + 1.0)
    ys_ref[...] = _dot(act, wd_ref[...]) + bd_ref[...]


def _expert_ffn(xs, blk_exp, w_gu, b_gu, w_down, b_down):
    n_rows, d = xs.shape
    ne, _, f2 = w_gu.shape
    ff = w_down.shape[1]
    grid_spec = pltpu.PrefetchScalarGridSpec(
        num_scalar_prefetch=1, grid=(n_rows // MOE_BM,),
        in_specs=[pl.BlockSpec((MOE_BM, d), lambda i, be: (i, 0)),
                  pl.BlockSpec((None, d, f2), lambda i, be: (be[i], 0, 0)),
                  pl.BlockSpec((None, 1, f2), lambda i, be: (be[i], 0, 0)),
                  pl.BlockSpec((None, ff, d), lambda i, be: (be[i], 0, 0)),
                  pl.BlockSpec((None, 1, d), lambda i, be: (be[i], 0, 0))],
        out_specs=pl.BlockSpec((MOE_BM, d), lambda i, be: (i, 0)))
    return pl.pallas_call(
        _expert_kernel, out_shape=jax.ShapeDtypeStruct((n_rows, d), F32), grid_spec=grid_spec,
        compiler_params=_cp(("arbitrary",)), name="moe_experts",
    )(blk_exp, xs, w_gu, b_gu.reshape(ne, 1, f2), w_down, b_down.reshape(ne, 1, d))


def _combine_kernel(nct, nb, nt, d, pos_ref, ri_ref, x1_ref, m_ref, ys_hbm, x2_ref, buf, sem):
    i = pl.program_id(0)
    b = i // nt
    t = i % nt
    row = jnp.where(t < nct, nb, b)

    def row_copy(r, k):
        return pltpu.make_async_copy(ys_hbm.at[pl.ds(pos_ref[TOP_K * r + k], 1)], buf.at[k, pl.ds(r, 1)], sem)

    def start(r, c):
        for k in range(TOP_K):
            row_copy(r, k).start()
        return c

    def wait(r, c):
        for k in range(TOP_K):
            row_copy(r, k).wait()
        return c

    lax.fori_loop(0, TM, start, 0)
    lax.fori_loop(0, TM, wait, 0)
    ri = ri_ref[...]
    y = None
    for k in range(TOP_K):
        term = buf[k] * ri[:, 2 * TOP_K + k:2 * TOP_K + k + 1]
        y = term if y is None else y + term
    x2_ref[...] = x1_ref[...] + _mod_row(m_ref, row, 5, d) * y


def _combine(ys, pos_flat, rinfo2, x1_2, mods, nct, nb):
    n, d = x1_2.shape
    nt = n // TM // nb
    return pl.pallas_call(
        functools.partial(_combine_kernel, nct, nb, nt, d),
        out_shape=jax.ShapeDtypeStruct((n, d), F32),
        grid=(n // TM,),
        in_specs=[pl.BlockSpec((TOP_K * TM,), lambda i: (i,), memory_space=pltpu.SMEM),
                  pl.BlockSpec((TM, LANES), lambda i: (i, 0)),
                  pl.BlockSpec((TM, d), lambda i: (i, 0)),
                  pl.BlockSpec(mods.shape, lambda i: (0, 0)),
                  pl.BlockSpec(memory_space=pl.ANY)],
        out_specs=pl.BlockSpec((TM, d), lambda i: (i, 0)),
        scratch_shapes=[pltpu.VMEM((TOP_K, TM, d), F32), pltpu.SemaphoreType.DMA],
        compiler_params=_cp(("arbitrary",)), name="moe_combine",
    )(pos_flat, rinfo2, x1_2, mods, ys)


def _moe(f, rinfo, counts, x1, mods, w_gu, b_gu, w_down, b_down, nct):
    nb, t_all, d = f.shape
    n = nb * t_all
    ne = w_gu.shape[0]
    ri2 = rinfo.reshape(n, LANES)
    idx = ri2[:, :TOP_K].astype(jnp.int32)
    rank = ri2[:, TOP_K:2 * TOP_K].astype(jnp.int32)
    cnt = counts[0, :ne].astype(jnp.int32)
    padded = (cnt + MOE_BM - 1) // MOE_BM * MOE_BM
    pad_end = jnp.cumsum(padded)
    pad_start = pad_end - padded
    onehot = idx[:, :, None] == jnp.arange(ne, dtype=jnp.int32)[None, None, :]
    pos = (rank + jnp.sum(jnp.where(onehot, pad_start[None, None, :], 0), axis=-1)).reshape(-1)
    n_blk = (n * TOP_K + MOE_BM - 1) // MOE_BM + ne
    blk_start = jnp.arange(n_blk, dtype=jnp.int32) * MOE_BM
    blk_exp = jnp.minimum(jnp.sum(blk_start[:, None] >= pad_end[None, :], axis=-1), ne - 1).astype(jnp.int32)
    xs = _dispatch(f.reshape(n, d), pos, n_blk * MOE_BM)
    ys = _expert_ffn(xs, blk_exp, w_gu, b_gu, w_down, b_down)
    x2 = _combine(ys, pos, ri2, x1.reshape(n, d), mods, nct, nb)
    return x2.reshape(nb, t_all, d)


def _final_kernel(x_ref, g_ref, o_ref):
    x = x_ref[...]
    o_ref[...] = x * lax.rsqrt(jnp.mean(x * x, axis=-1, keepdims=True) + NORM_EPS) * g_ref[...]


def _final_norm(xt, g, nct):
    nb, t_all, d = xt.shape
    seq = t_all - nct * TM
    return pl.pallas_call(
        _final_kernel, out_shape=jax.ShapeDtypeStruct((nb, seq, d), F32), grid=(nb, seq // TM),
        in_specs=[pl.BlockSpec((None, TM, d), lambda b, t: (b, t + nct, 0)), pl.BlockSpec(g.shape, lambda b, t: (0, 0))],
        out_specs=pl.BlockSpec((None, TM, d), lambda b, t: (b, t, 0)),
        compiler_params=_cp(("arbitrary", "arbitrary")), name="final_norm",
    )(xt, g)


def _rope_tables(seq, lc):
    rows = seq // GRID_W
    row = jnp.repeat(jnp.arange(rows), GRID_W).astype(F32)
    col = jnp.tile(jnp.arange(GRID_W), rows).astype(F32)
    inv = ROPE_BASE ** (-jnp.arange(16, dtype=F32) / 16)
    ang_r = row[:, None] * inv[None, :]
    ang_c = col[:, None] * inv[None, :]
    cos = jnp.concatenate([jnp.cos(ang_r)] * 2 + [jnp.cos(ang_c)] * 2, axis=1)
    sin = jnp.concatenate([-jnp.sin(ang_r), jnp.sin(ang_r), -jnp.sin(ang_c), jnp.sin(ang_c)], axis=1)
    cos = jnp.concatenate([jnp.ones((lc, 64), F32), cos], axis=0)
    sin = jnp.concatenate([jnp.zeros((lc, 64), F32), sin], axis=0)
    return jnp.tile(cos, (1, 2)), jnp.tile(sin, (1, 2))


def _pad_lanes(a, fill=0.0):
    return jnp.pad(a, [(0, 0)] * (a.ndim - 1) + [(0, LANES - a.shape[-1])], constant_values=fill)


def _gate_consts(first, second):
    return _pad_lanes(jnp.concatenate([first.reshape(1, -1), second.reshape(1, -1)], axis=1))


def kernel(x, c, ctx, c_ctx, ada_w, ada_b, norm_mix_g, norm_ffn_g, final_g, gdn_w_in, gdn_conv_w, gdn_a_log, gdn_dt_bias, gdn_norm_g, gdn_w_out, mlstm_w_in, mlstm_b_i, mlstm_b_f, mlstm_norm_g, mlstm_w_out, diff_w_in, diff_lambda, diff_norm_g, diff_w_out, swa_w_in, swa_sink, swa_w_out, router_w, router_b, moe_w_gu, moe_b_gu, moe_w_down, moe_b_down):
    nb, seq, d = x.shape
    lc = ctx.shape[1]
    depth = ada_w.shape[0]
    assert lc % TM == 0 and seq % TM == 0 and nb < 8
    nct = lc // TM
    xt = jnp.concatenate([ctx, x], axis=1)
    cond8 = jnp.zeros((8, d), F32).at[:nb].set(c).at[nb].set(c_ctx)
    mods_all = _ada_mods(cond8, ada_w, ada_b)
    cos, sin = _rope_tables(seq, lc)
    for i in range(depth):
        kind, j = i % 4, i // 4
        mods = mods_all[i]
        gm = norm_mix_g[i].reshape(1, d)
        gf = norm_ffn_g[i].reshape(1, d)
        rw = _pad_lanes(router_w[i])
        rb = _pad_lanes(router_b[i].reshape(1, -1), -jnp.inf)
        if kind == 0:
            w = gdn_w_in[j]
            main = w.shape[1] - 32
            p, gt, gc = _inproj(0, xt, gm, mods, w[:, :main].astype(BF16),
                                (_pad_lanes(w[:, main:]), _gate_consts(gdn_a_log[j], jnp.zeros_like(gdn_a_log[j])),
                                 _gate_consts(gdn_dt_bias[j], jnp.zeros_like(gdn_dt_bias[j]))), nct)
            q, k, v = _gdn_prep(p, jnp.pad(gdn_conv_w[j], ((0, 3), (0, 0))), nct)
            of, ob = _gdn_scan(q, k, v, gt, gc, nct)
            mixer_out = (of, ob, p, 3 * d, gdn_norm_g[j].reshape(1, -1))
            w_out, scale = gdn_w_out[j], 1.0
        elif kind == 1:
            w = mlstm_w_in[j]
            main = w.shape[1] - 32
            p, gt, gc = _inproj(1, xt, gm, mods, w[:, :main].astype(BF16),
                                (_pad_lanes(w[:, main:]), _gate_consts(mlstm_b_i[j], jnp.zeros_like(mlstm_b_i[j])),
                                 _pad_lanes(jnp.concatenate([jnp.zeros((1, 16), F32), mlstm_b_f[j].reshape(1, -1)], axis=1))),
                                nct)
            of, ob = _mlstm_scan(p[..., :d // 2], p[..., d // 2:d], p[..., d:2 * d], gt, gc, nct)
            mixer_out = (of, ob, p, 2 * d, mlstm_norm_g[j].reshape(1, -1))
            w_out, scale = mlstm_w_out[j], 1.0
        elif kind == 2:
            lambda_init = 0.8 - 0.6 * math.exp(-0.3 * i)
            q, k, v = _inproj(2, xt, gm, mods, diff_w_in[j].astype(BF16), (cos, sin, (d, d, d)), nct)
            o = _diff_attn(q, k, v, diff_lambda[j], lambda_init, nct)
            mixer_out = (o, diff_norm_g[j].reshape(1, -1))
            w_out, scale = diff_w_out[j], 1.0 - lambda_init
        else:
            q, k, v = _inproj(3, xt, gm, mods, swa_w_in[j].astype(BF16), (cos, sin, (d, d // 4, d // 4)), nct)
            mixer_out = _swa_attn(q, k, v, _pad_lanes(swa_sink[j].reshape(1, -1)), nct, seq)
            w_out, scale = swa_w_out[j], 1.0
        x1, f, rinfo, counts = _outproj(kind, mixer_out, xt, mods, w_out.astype(BF16), gf, rw, rb, nct, scale)
        xt = _moe(f, rinfo, counts, x1, mods, moe_w_gu[i].astype(BF16), moe_b_gu[i], moe_w_down[i].astype(BF16),
                  moe_b_down[i], nct)
    return _final_norm(xt, final_g.reshape(1, d), nct)
```

```python
import functools
import math

import jax
import jax.numpy as jnp
from jax import lax
from jax.experimental import pallas as pl
from jax.experimental.pallas import tpu as pltpu

F32 = jnp.float32
BF16 = jnp.bfloat16
HI = lax.Precision.HIGHEST

NORM_EPS = 1e-6
ROPE_BASE = 10000.0
GRID_W = 64
TM = 256
CHUNK = 64
CHUNK_SHIFT = 6
GDN_HEADS_PER_STEP = 2
Q_SCALE = {2: 0.125 * math.log2(math.e), 3: 0.125}
LANES = 128
N_EXPERTS = 32
TOP_K = 4
SWIGLU_ALPHA = 1.702
SWIGLU_LIMIT = 7.0
MOE_BM = 256
SWA_WINDOW = 128
VMEM_LIMIT = 56 * 1024 * 1024


def _cp(sem, vmem=VMEM_LIMIT):
    return pltpu.CompilerParams(dimension_semantics=sem, vmem_limit_bytes=vmem)


def _dot(a, b):
    return jnp.dot(a.astype(BF16), b.astype(BF16), preferred_element_type=F32)


def _dot_nt(a, b):
    return lax.dot_general(a.astype(BF16), b.astype(BF16), (((1,), (1,)), ((), ())),
                           preferred_element_type=F32)


def _dot_tn(a, b):
    return lax.dot_general(a.astype(BF16), b.astype(BF16), (((0,), (0,)), ((), ())),
                           preferred_element_type=F32)


def _dot_hi(a, b):
    return jnp.dot(a, b, precision=HI, preferred_element_type=F32)


def _sigmoid(x):
    return 1.0 / (1.0 + jnp.exp(-x))


def _softplus(x):
    return jnp.maximum(x, 0.0) + jnp.log(1.0 + jnp.exp(-jnp.abs(x)))


def _mods_kernel(c_ref, w_ref, b_ref, o_ref):
    c = c_ref[...]
    o_ref[...] = _dot_hi(c * _sigmoid(c), w_ref[...]) + b_ref[...]


def _ada_mods(cond8, ada_w, ada_b):
    depth, d, d6 = ada_w.shape
    return pl.pallas_call(
        _mods_kernel,
        out_shape=jax.ShapeDtypeStruct((depth, 8, d6), F32),
        grid=(depth, d6 // d),
        in_specs=[pl.BlockSpec((8, d), lambda i, j: (0, 0)),
                  pl.BlockSpec((None, d, d), lambda i, j: (i, 0, j)),
                  pl.BlockSpec((None, 1, d), lambda i, j: (i, 0, j))],
        out_specs=pl.BlockSpec((None, 8, d), lambda i, j: (i, 0, j)),
        compiler_params=_cp(("arbitrary", "arbitrary")),
        name="ada_mods",
    )(cond8, ada_w, ada_b.reshape(depth, 1, d6))


def _mod_row(m_ref, row, k, d):
    return m_ref[pl.ds(row, 1), k * d:(k + 1) * d]


def _prenorm_mod(x, g, shift, scale):
    ms = jnp.mean(x * x, axis=-1, keepdims=True)
    y = x * lax.rsqrt(ms + NORM_EPS) * g
    return y * (1.0 + scale) + shift


def _rope_slab(x, cos, sin, first_half):
    part = jnp.where(first_half, pltpu.roll(x, LANES - 16, 1), pltpu.roll(x, 16, 1))
    return x * cos + part * sin


def _inproj_kernel(kind, nct, nb, d, *refs):
    b = pl.program_id(0)
    t = pl.program_id(1)
    row = jnp.where(t < nct, nb, b)
    x_ref, g_ref, m_ref, w_ref = refs[:4]
    rest = refs[4:]
    h = _prenorm_mod(x_ref[...], g_ref[...], _mod_row(m_ref, row, 0, d), _mod_row(m_ref, row, 1, d))
    p = _dot(h, w_ref[...])
    if kind in (0, 1):
        wg_ref, c0_ref, c1_ref, p_ref, gt_ref, gc_ref = rest
        p_ref[...] = p
        graw = _dot_hi(h, wg_ref[...])
        lane = lax.broadcasted_iota(jnp.int32, graw.shape, 1)
        if kind == 0:
            gate = jnp.where(lane < 16, -jnp.exp(c0_ref[...]) * _softplus(graw + c1_ref[...]),
                             _sigmoid(graw))
        else:
            gate = jnp.where(lane < 16, graw + c0_ref[...], -_softplus(-(graw + c1_ref[...])))
        ii = lax.broadcasted_iota(jnp.int32, (TM, TM), 0)
        jj = lax.broadcasted_iota(jnp.int32, (TM, TM), 1)
        same = jnp.right_shift(ii, CHUNK_SHIFT) == jnp.right_shift(jj, CHUNK_SHIFT)
        cum_f = _dot_hi(jnp.where(same, jnp.where(jj <= ii, 1.0, 0.0), 0.0), gate)
        cum_b = _dot_hi(jnp.where(same, jnp.where(jj >= ii, 1.0, 0.0), 0.0), gate)
        cum = jnp.where(lane % 16 < 8, cum_f, cum_b)
        decay_col = jnp.right_shift(lane, 4) == kind
        gate = jnp.where(decay_col, cum, gate)
        gt_ref[...] = gate
        gc_ref[...] = gate.T
    else:
        cos_ref, sin_ref, q_ref, k_ref, v_ref = rest
        cos = cos_ref[...]
        sin = sin_ref[...]
        lane = lax.broadcasted_iota(jnp.int32, cos.shape, 1)
        first_half = (lane % 32) < 16
        nq = q_ref.shape[-1] // LANES
        nk = k_ref.shape[-1] // LANES
        for s in range(nq):
            q_ref[:, s * LANES:(s + 1) * LANES] = (
                _rope_slab(p[:, s * LANES:(s + 1) * LANES], cos, sin, first_half) * Q_SCALE[kind]).astype(BF16)
        for s in range(nk):
            o = (nq + s) * LANES
            k_ref[:, s * LANES:(s + 1) * LANES] = _rope_slab(p[:, o:o + LANES], cos, sin, first_half).astype(BF16)
        o = (nq + nk) * LANES
        v_ref[...] = p[:, o:].astype(BF16)


def _inproj(kind, xt, g, mods, w_main, extra, nct):
    nb, t_all, d = xt.shape
    nt = t_all // TM
    pdim = w_main.shape[1]
    tile = lambda w: pl.BlockSpec((None, TM, w), lambda b, t: (b, t, 0))
    full = lambda a: pl.BlockSpec(a.shape, lambda b, t: (0,) * a.ndim)
    in_specs = [tile(d), full(g), full(mods), full(w_main)]
    if kind in (0, 1):
        wg, c0, c1 = extra
        in_specs += [full(wg), full(c0), full(c1)]
        out_shape = (jax.ShapeDtypeStruct((nb, t_all, pdim), F32),
                     jax.ShapeDtypeStruct((nb, t_all, LANES), F32),
                     jax.ShapeDtypeStruct((nb, LANES, t_all), F32))
        out_specs = (tile(pdim), tile(LANES), pl.BlockSpec((None, LANES, TM), lambda b, t: (b, 0, t)))
        args = (xt, g, mods, w_main, wg, c0, c1)
    else:
        cos, sin, widths = extra
        in_specs += [pl.BlockSpec((TM, LANES), lambda b, t: (t, 0))] * 2
        out_shape = tuple(jax.ShapeDtypeStruct((nb, t_all, w), BF16) for w in widths)
        out_specs = tuple(tile(w) for w in widths)
        args = (xt, g, mods, w_main, cos, sin)
    return pl.pallas_call(
        functools.partial(_inproj_kernel, kind, nct, nb, d),
        out_shape=out_shape, grid=(nb, nt), in_specs=in_specs, out_specs=out_specs,
        compiler_params=_cp(("arbitrary", "arbitrary")), name=f"inproj{kind}",
    )(*args)


def _gdn_prep_kernel(nct, nt, p_ref, prev_ref, next_ref, w_ref, q_ref, k_ref, v_ref, win_ref):
    t = pl.program_id(1)
    prev_ok = jnp.logical_and(t != 0, t != nct)
    next_ok = jnp.logical_and(t != nct - 1, t != nt - 1)
    win_ref[0:8, :] = jnp.where(prev_ok, prev_ref[...], 0.0)
    win_ref[8:8 + TM, :] = p_ref[...]
    win_ref[8 + TM:, :] = jnp.where(next_ok, next_ref[...], 0.0)
    width = q_ref.shape[-1]
    for part, o_ref in enumerate((q_ref, k_ref, v_ref)):
        for h in range(width // LANES):
            c0 = part * width + h * LANES
            acc = None
            for j in range(5):
                term = win_ref[6 + j:6 + j + TM, c0:c0 + LANES] * w_ref[j:j + 1, c0:c0 + LANES]
                acc = term if acc is None else acc + term
            y = acc * _sigmoid(acc)
            if part < 2:
                y = y * lax.rsqrt(jnp.sum(y * y, axis=-1, keepdims=True) + NORM_EPS)
            if part == 0:
                y = y * (LANES ** -0.5)
            o_ref[:, h * LANES:(h + 1) * LANES] = y


def _gdn_prep(p_main, conv_w8, nct):
    nb, t_all, _ = p_main.shape
    nt = t_all // TM
    width = conv_w8.shape[1] // 3
    cw = 3 * width
    t8 = t_all // 8
    r8 = TM // 8
    tile = pl.BlockSpec((None, TM, width), lambda b, t: (b, t, 0))
    return pl.pallas_call(
        functools.partial(_gdn_prep_kernel, nct, nt),
        out_shape=tuple(jax.ShapeDtypeStruct((nb, t_all, width), F32) for _ in range(3)),
        grid=(nb, nt),
        in_specs=[pl.BlockSpec((None, TM, cw), lambda b, t: (b, t, 0)),
                  pl.BlockSpec((None, 8, cw), lambda b, t: (b, jnp.maximum(t * r8 - 1, 0), 0)),
                  pl.BlockSpec((None, 8, cw), lambda b, t: (b, jnp.minimum((t + 1) * r8, t8 - 1), 0)),
                  pl.BlockSpec(conv_w8.shape, lambda b, t: (0, 0))],
        out_specs=(tile, tile, tile),
        scratch_shapes=[pltpu.VMEM((TM + 16, cw), F32)],
        compiler_params=_cp(("arbitrary", "arbitrary")), name="gdn_prep",
    )(p_main, p_main, p_main, conv_w8)


def _split_bf16(a):
    hi = a.astype(BF16)
    return hi, (a - hi.astype(F32)).astype(BF16)


def _dot3(a, b):
    ah, al = _split_bf16(a)
    bh, bl = _split_bf16(b)
    d = lambda x, y: jnp.dot(x, y, preferred_element_type=F32)
    return d(ah, bh) + (d(ah, bl) + d(al, bh))


def _unit_triangular_inverses(mats, same):
    a8 = [jnp.where(same(3), a, 0.0) for a in mats]
    ts = [jnp.where(same(0), 1.0, 0.0) - x for x in a8]
    p2 = [_dot3(x, x) for x in a8]
    ts = [t + _dot3(t, p) for t, p in zip(ts, p2)]
    p4 = [_dot3(p, p) for p in p2]
    ts = [t + _dot3(t, p) for t, p in zip(ts, p4)]
    for sh in (4, 5, 6):
        offs = [jnp.where(same(sh), jnp.where(same(sh - 1), 0.0, a), 0.0) for a in mats]
        mids = [_dot3(o, t) for o, t in zip(offs, ts)]
        ts = [t - _dot3(t, m) for t, m in zip(ts, mids)]
    return ts


def _gdn_tiles(probs):
    ii = lax.broadcasted_iota(jnp.int32, (TM, TM), 0)
    jj = lax.broadcasted_iota(jnp.int32, (TM, TM), 1)
    same = lambda sh: jnp.right_shift(ii, sh) == jnp.right_shift(jj, sh)
    incl = {rev: jnp.where(same(CHUNK_SHIFT), jnp.where((ii <= jj) if rev else (ii >= jj), 1.0, 0.0), 0.0) > 0.5
            for rev in {p[8] for p in probs}}
    gams = [jnp.where(incl[rev], jnp.exp(jnp.where(incl[rev], gc - gr, 0.0)), 0.0)
            for (_, _, _, _, gc, gr, _, _, rev) in probs]
    kbs = [p[1] * p[3] for p in probs]
    mats = [jnp.where(same(0), 0.0, _dot_nt(kb, p[1]) * gam) for kb, p, gam in zip(kbs, probs, gams)]
    tinvs = _unit_triangular_inverses(mats, same)
    egs = [jnp.exp(p[4]) for p in probs]
    uws = [_dot3(t, jnp.concatenate([p[2] * p[3], kb * eg], axis=1)) for t, p, kb, eg in zip(tinvs, probs, kbs, egs)]
    qks = [_dot_nt(p[0], p[1]) * gam for p, gam in zip(probs, gams)]
    qgs = [p[0] * eg for p, eg in zip(probs, egs)]
    states = [p[6] for p in probs]
    nchunk = TM // CHUNK
    for n in range(nchunk):
        cs = [nchunk - 1 - n if p[8] else n for p in probs]
        sls = [slice(c * CHUNK, (c + 1) * CHUNK) for c in cs]
        gtots = [p[4][(c * CHUNK if p[8] else (c + 1) * CHUNK - 1):][:1] for p, c in zip(probs, cs)]
        wss = [_dot(uw[sl, LANES:], s) for uw, sl, s in zip(uws, sls, states)]
        qss = [_dot(qg[sl], s) for qg, sl, s in zip(qgs, sls, states)]
        vns = [uw[sl, :LANES] - ws for uw, sl, ws in zip(uws, sls, wss)]
        for p, sl, qs, qk, vn in zip(probs, sls, qss, qks, vns):
            p[7](sl, qs + _dot(qk[sl, sl], vn))
        states = [s * jnp.exp(gt) + _dot_tn(p[1][sl] * jnp.exp(gt - p[4][sl]), vn)
                  for s, gt, p, sl, vn in zip(states, gtots, probs, sls, vns)]
    return states


def _gate_col(gt, idx):
    lane = lax.broadcasted_iota(jnp.int32, gt.shape, 1)
    return jnp.sum(jnp.where(lane == idx, gt, 0.0), axis=1, keepdims=True)


def _gate_row(gc, idx):
    sub = lax.broadcasted_iota(jnp.int32, gc.shape, 0)
    return jnp.sum(jnp.where(sub == idx, gc, 0.0), axis=0, keepdims=True)


def _gdn_scan_kernel(qf, kf, vf, gtf, gcf, qb, kb, vb, gtb, gcb, of_ref, ob_ref, s_ref):
    hp = pl.program_id(1)

    @pl.when(pl.program_id(2) == 0)
    def _():
        s_ref[...] = jnp.zeros_like(s_ref)

    probs = []
    for rev, (q_r, k_r, v_r, gt_r, gc_r, o_r) in enumerate(
            ((qf, kf, vf, gtf, gcf, of_ref), (qb, kb, vb, gtb, gcb, ob_ref))):
        gt = gt_r[...]
        gc = gc_r[...]
        for j in range(GDN_HEADS_PER_STEP):
            head = hp * GDN_HEADS_PER_STEP + j
            cols = slice(j * LANES, (j + 1) * LANES)

            def store(sl, val, o_r=o_r, cols=cols):
                o_r[sl, cols] = val

            probs.append((q_r[:, cols], k_r[:, cols], v_r[:, cols], _gate_col(gt, 16 + rev * 8 + head),
                          _gate_col(gt, rev * 8 + head), _gate_row(gc, rev * 8 + head), s_ref[rev, j], store,
                          bool(rev)))
    states = _gdn_tiles(probs)
    for rev in range(2):
        for j in range(GDN_HEADS_PER_STEP):
            s_ref[rev, j] = states[rev * GDN_HEADS_PER_STEP + j]


def _scan_tiles(nct, nt):
    fwd = lambda s: s
    bwd = lambda s: jnp.where(s < nct, nct - 1 - s, nt - 1 - (s - nct))
    return fwd, bwd


def _gdn_scan(q, k, v, gt, gc, nct):
    nb, t_all, width = q.shape
    hw = GDN_HEADS_PER_STEP * LANES
    nt = t_all // TM
    fwd, bwd = _scan_tiles(nct, nt)

    def specs(tile_of):
        head = pl.BlockSpec((None, TM, hw), lambda b, h, s: (b, tile_of(s), h))
        return [head, head, head,
                pl.BlockSpec((None, TM, LANES), lambda b, h, s: (b, tile_of(s), 0)),
                pl.BlockSpec((None, LANES, TM), lambda b, h, s: (b, 0, tile_of(s)))]

    out = jax.ShapeDtypeStruct((nb, t_all, width), F32)
    return pl.pallas_call(
        _gdn_scan_kernel, out_shape=(out, out), grid=(nb, width // hw, nt),
        in_specs=specs(fwd) + specs(bwd),
        out_specs=(pl.BlockSpec((None, TM, hw), lambda b, h, s: (b, fwd(s), h)),
                   pl.BlockSpec((None, TM, hw), lambda b, h, s: (b, bwd(s), h))),
        scratch_shapes=[pltpu.VMEM((2, GDN_HEADS_PER_STEP, LANES, LANES), F32)],
        compiler_params=_cp(("arbitrary", "arbitrary", "arbitrary")), name="gdn_scan",
    )(q, k, v, gt, gc, q, k, v, gt, gc)


def _mlstm_tiles(probs):
    ii = lax.broadcasted_iota(jnp.int32, (TM, TM), 0)
    jj = lax.broadcasted_iota(jnp.int32, (TM, TM), 1)
    same_chunk = jnp.right_shift(ii, CHUNK_SHIFT) == jnp.right_shift(jj, CHUNK_SHIFT)
    incl = {rev: jnp.where(same_chunk, jnp.where((ii <= jj) if rev else (ii >= jj), 1.0, 0.0), 0.0) > 0.5
            for rev in {p[10] for p in probs}}
    nchunk = TM // CHUNK
    order = lambda p: [nchunk - 1 - n if p[10] else n for n in range(nchunk)]
    rows = lambda c: slice(c * CHUNK, (c + 1) * CHUNK)
    dlogs = [jnp.where(incl[p[10]], p[5] - p[6] + p[4], -jnp.inf) for p in probs]
    m_intras = [jnp.max(d, axis=-1, keepdims=True) for d in dlogs]
    qks = [_dot_nt(p[0], p[1]) for p in probs]
    btots, m_locs, c_locs = [], [], []
    for p in probs:
        bt, ml, cl = {}, {}, {}
        for c in order(p):
            last = c * CHUNK if p[10] else (c + 1) * CHUNK - 1
            bt[c] = p[5][last:last + 1]
            a_loc = bt[c] - p[5][rows(c)] + p[3][rows(c)]
            ml[c] = jnp.max(a_loc, axis=0, keepdims=True)
            cl[c] = _dot_tn(p[1][rows(c)] * jnp.exp(a_loc - ml[c]), p[2][rows(c)])
        btots.append(bt)
        m_locs.append(ml)
        c_locs.append(cl)
    new_states, c_ins, m_ins = [], [], []
    for p, bt, ml, cl in zip(probs, btots, m_locs, c_locs):
        c_st, m_st = p[7], p[8]
        ci, mi = {}, {}
        for c in order(p):
            ci[c], mi[c] = c_st, m_st
            m_new = jnp.maximum(bt[c] + m_st, ml[c])
            c_st = c_st * jnp.exp(bt[c] + m_st - m_new) + cl[c] * jnp.exp(ml[c] - m_new)
            m_st = m_new
        new_states.append((c_st, m_st))
        c_ins.append(ci)
        m_ins.append(mi)
    inters = [jnp.concatenate([_dot(p[0][rows(c)], ci[c]) for c in range(nchunk)], axis=0)
              for p, ci in zip(probs, c_ins)]
    for p, dlog, m_intra, qk, inter, mi in zip(probs, dlogs, m_intras, qks, inters, m_ins):
        m_in = jnp.concatenate([jnp.broadcast_to(mi[c], (CHUNK, 1)) for c in range(nchunk)], axis=0)
        log_inter = p[5] + m_in
        m_t = jnp.maximum(log_inter, m_intra)
        num = jnp.exp(log_inter - m_t) * inter + _dot(jnp.exp(dlog - m_t) * qk, p[2])
        dv = p[2].shape[1] // 2
        p[9](num[:, :dv] / jnp.maximum(jnp.abs(num[:, dv:dv + 1]), jnp.exp(-m_t)))
    return new_states


def _mlstm_scan_kernel(qf, kf, vf, gtf, gcf, qb, kb, vb, gtb, gcb, of_ref, ob_ref, c_ref, m_ref):
    hp = pl.program_id(1)

    @pl.when(pl.program_id(2) == 0)
    def _():
        c_ref[...] = jnp.zeros_like(c_ref)
        m_ref[...] = jnp.zeros_like(m_ref)

    dqk = qf.shape[-1] // 2
    dv = vf.shape[-1] // 2
    lane = lax.broadcasted_iota(jnp.int32, (TM, dv), 1)
    ones_pad = jnp.where(lane == 0, 1.0, 0.0)
    probs = []
    for rev, (q_r, k_r, v_r, gt_r, gc_r, o_r) in enumerate(
            ((qf, kf, vf, gtf, gcf, of_ref), (qb, kb, vb, gtb, gcb, ob_ref))):
        gt = gt_r[...]
        gc = gc_r[...]
        for j in range(2):
            head = 2 * hp + j

            def store(val, o_r=o_r, j=j):
                o_r[:, j * dv:(j + 1) * dv] = val

            probs.append((q_r[:, j * dqk:(j + 1) * dqk], k_r[:, j * dqk:(j + 1) * dqk] * (dqk ** -0.5),
                          jnp.concatenate([v_r[:, j * dv:(j + 1) * dv], ones_pad], axis=1),
                          _gate_col(gt, rev * 8 + head), _gate_row(gc, rev * 8 + head),
                          _gate_col(gt, 16 + rev * 8 + head), _gate_row(gc, 16 + rev * 8 + head),
                          c_ref[rev, j], m_ref[rev, j][0:1, 0:1], store, bool(rev)))
    states = _mlstm_tiles(probs)
    for rev in range(2):
        for j in range(2):
            c_st, m_st = states[rev * 2 + j]
            c_ref[rev, j] = c_st
            m_ref[rev, j] = jnp.broadcast_to(m_st, (8, LANES))


def _mlstm_scan(q, k, v, gt, gc, nct):
    nb, t_all, qw = q.shape
    vw = v.shape[-1]
    nhp = qw // LANES
    nt = t_all // TM
    dqk = LANES // 2
    dv = vw // nhp // 2
    fwd, bwd = _scan_tiles(nct, nt)

    def specs(tile_of):
        qk = pl.BlockSpec((None, TM, LANES), lambda b, h, s: (b, tile_of(s), h))
        return [qk, qk,
                pl.BlockSpec((None, TM, 2 * dv), lambda b, h, s: (b, tile_of(s), h)),
                pl.BlockSpec((None, TM, LANES), lambda b, h, s: (b, tile_of(s), 0)),
                pl.BlockSpec((None, LANES, TM), lambda b, h, s: (b, 0, tile_of(s)))]

    out = jax.ShapeDtypeStruct((nb, t_all, vw), F32)
    return pl.pallas_call(
        _mlstm_scan_kernel, out_shape=(out, out), grid=(nb, nhp, nt),
        in_specs=specs(fwd) + specs(bwd),
        out_specs=(pl.BlockSpec((None, TM, 2 * dv), lambda b, h, s: (b, fwd(s), h)),
                   pl.BlockSpec((None, TM, 2 * dv), lambda b, h, s: (b, bwd(s), h))),
        scratch_shapes=[pltpu.VMEM((2, 2, dqk, 2 * dv), F32), pltpu.VMEM((2, 2, 8, LANES), F32)],
        compiler_params=_cp(("arbitrary", "arbitrary", "arbitrary")), name="mlstm_scan",
    )(q, k, v, gt, gc, q, k, v, gt, gc)


def _diff_attn_kernel(nk, lambda_init, lam_ref, q_ref, k_ref, v_ref, o_ref, m_ref, l_ref, acc_ref):
    j = pl.program_id(3)
    dh = LANES // 2

    @pl.when(j == 0)
    def _():
        m_ref[...] = jnp.full(m_ref.shape, -jnp.inf, F32)
        l_ref[...] = jnp.zeros_like(l_ref)
        acc_ref[...] = jnp.zeros_like(acc_ref)

    v = v_ref[...]
    tq = q_ref.shape[0]
    nsplit = 2 if tq % 512 == 0 else 1
    rs = tq // nsplit
    probs = [(hh, slice(r * rs, (r + 1) * rs)) for r in range(nsplit) for hh in range(2)]
    scores = [lax.dot_general(q_ref[rows, hh * dh:(hh + 1) * dh], k_ref[:, hh * dh:(hh + 1) * dh],
                              (((1,), (1,)), ((), ())), preferred_element_type=F32) for hh, rows in probs]
    for (hh, rows), s in zip(probs, scores):
        m_old = m_ref[hh, rows, 0:1]
        m_new = jnp.maximum(m_old, jnp.max(s, axis=-1, keepdims=True))
        alpha = jnp.exp2(m_old - m_new)
        p = jnp.exp2(s - m_new)
        l_new = alpha * l_ref[hh, rows, 0:1] + jnp.sum(p, axis=-1, keepdims=True)
        acc_ref[hh, rows, :] = alpha * acc_ref[hh, rows, :] + jnp.dot(p.astype(BF16), v, preferred_element_type=F32)
        m_ref[hh, rows, :] = jnp.broadcast_to(m_new, (rs, LANES))
        l_ref[hh, rows, :] = jnp.broadcast_to(l_new, (rs, LANES))

    @pl.when(j == nk - 1)
    def _():
        lp = lam_ref[...]
        lam = (jnp.exp(jnp.sum(lp[0:1] * lp[1:2], axis=-1, keepdims=True))
               - jnp.exp(jnp.sum(lp[2:3] * lp[3:4], axis=-1, keepdims=True)) + lambda_init)
        o_ref[...] = acc_ref[0] / l_ref[0][:, 0:1] - lam * (acc_ref[1] / l_ref[1][:, 0:1])


def _pick_tile(n, candidates):
    return next(c for c in candidates if n % c == 0)


def _diff_attn(q, k, v, lam_p, lambda_init):
    nb, nq_rows, width = q.shape
    ns = k.shape[1]
    tq = _pick_tile(nq_rows, (1024, 512, 256))
    tk = _pick_tile(ns, (1280, 1024, 768, 512, 256))
    nk = ns // tk
    kv = pl.BlockSpec((None, tk, LANES), lambda b, h, i, j: (b, j, h))
    qo = pl.BlockSpec((None, tq, LANES), lambda b, h, i, j: (b, i, h))
    return pl.pallas_call(
        functools.partial(_diff_attn_kernel, nk, lambda_init),
        out_shape=jax.ShapeDtypeStruct((nb, nq_rows, width), F32),
        grid=(nb, width // LANES, nq_rows // tq, nk),
        in_specs=[pl.BlockSpec(lam_p.shape, lambda b, h, i, j: (0, 0)), qo, kv, kv],
        out_specs=qo,
        scratch_shapes=[pltpu.VMEM((2, tq, LANES), F32), pltpu.VMEM((2, tq, LANES), F32),
                        pltpu.VMEM((2, tq, LANES), F32)],
        compiler_params=_cp(("arbitrary", "arbitrary", "arbitrary", "arbitrary")), name="diff_attn",
    )(lam_p, q, k, v)


def _swa_kernel(nct, nt, seq, sink_ref, q_ref, kc_ref, kp_ref, k0_ref, kn_ref, vc_ref, vp_ref, v0_ref, vn_ref,
                o_ref):
    i = pl.program_id(1)
    dh = LANES // 2
    nkv = kc_ref.shape[-1] // dh
    group = (q_ref.shape[-1] // dh) // nkv
    lc = kc_ref.shape[0]
    col = lax.broadcasted_iota(jnp.int32, (TM, lc + 3 * TM), 1)
    qpos = (i - nct) * TM + lax.broadcasted_iota(jnp.int32, (TM, lc + 3 * TM), 0)
    kpos = (i - nct - 1) * TM + col - lc
    ninf = -jnp.inf
    in_window = jnp.where(jnp.abs(qpos - kpos) <= SWA_WINDOW,
                          jnp.where(kpos >= 0, jnp.where(kpos < seq, 0.0, ninf), ninf), ninf)
    bias = jnp.where(col < lc, 0.0, jnp.where(i >= nct, in_window, ninf))
    kcat = jnp.concatenate([kc_ref[...], kp_ref[...], k0_ref[...], kn_ref[...]], axis=0)
    vcat = jnp.concatenate([vc_ref[...], vp_ref[...], v0_ref[...], vn_ref[...]], axis=0)
    sink = sink_ref[...]
    for g in range(nkv):
        kg = kcat[:, g * dh:(g + 1) * dh]
        vg = vcat[:, g * dh:(g + 1) * dh]
        for r in range(group):
            hq = g * group + r
            s = lax.dot_general(q_ref[:, hq * dh:(hq + 1) * dh], kg, (((1,), (1,)), ((), ())),
                                preferred_element_type=F32)
            s = s + bias
            sb = sink[0:1, hq:hq + 1]
            m = jnp.maximum(jnp.max(s, axis=-1, keepdims=True), sb)
            e = jnp.exp(s - m)
            p = e / (jnp.sum(e, axis=-1, keepdims=True) + jnp.exp(sb - m))
            o_ref[:, hq * dh:(hq + 1) * dh] = jnp.dot(p.astype(BF16), vg, preferred_element_type=F32)


def _swa_attn(q, k, v, sink_row, nct, seq):
    nb, t_all, qw = q.shape
    kw = k.shape[-1]
    nt = t_all // TM
    lc = nct * TM
    ctxs = pl.BlockSpec((None, lc, kw), lambda b, i: (b, 0, 0))
    prev = pl.BlockSpec((None, TM, kw), lambda b, i: (b, jnp.maximum(i - 1, 0), 0))
    cur = pl.BlockSpec((None, TM, kw), lambda b, i: (b, i, 0))
    nxt = pl.BlockSpec((None, TM, kw), lambda b, i: (b, jnp.minimum(i + 1, nt - 1), 0))
    qo = pl.BlockSpec((None, TM, qw), lambda b, i: (b, i, 0))
    return pl.pallas_call(
        functools.partial(_swa_kernel, nct, nt, seq),
        out_shape=jax.ShapeDtypeStruct((nb, t_all, qw), F32),
        grid=(nb, nt),
        in_specs=[pl.BlockSpec(sink_row.shape, lambda b, i: (0, 0)), qo, ctxs, prev, cur, nxt, ctxs, prev, cur, nxt],
        out_specs=qo,
        compiler_params=_cp(("arbitrary", "arbitrary")), name="swa_attn",
    )(sink_row, q, k, k, k, k, v, v, v, v)


def _headnorm(o, g_row):
    outs = []
    for h in range(o.shape[-1] // LANES):
        seg = o[:, h * LANES:(h + 1) * LANES]
        ms = jnp.mean(seg * seg, axis=-1, keepdims=True)
        outs.append(seg * lax.rsqrt(ms + NORM_EPS) * g_row)
    return jnp.concatenate(outs, axis=1)


def _outproj_kernel(kind, nct, nb, d, out_scale, *refs):
    b = pl.program_id(0)
    t = pl.program_id(1)
    row = jnp.where(t < nct, nb, b)
    if kind in (0, 1):
        of_ref, ob_ref, z_ref, hg_ref = refs[:4]
        rest = refs[4:]
        z = z_ref[...]
        o = _headnorm(of_ref[...] + ob_ref[...], hg_ref[...]) * (z * _sigmoid(z) if kind == 0 else _sigmoid(z))
    elif kind == 2:
        o_ref, hg_ref = refs[:2]
        rest = refs[2:]
        o = _headnorm(o_ref[...], hg_ref[...]) * out_scale
    else:
        o_ref = refs[0]
        rest = refs[1:]
        o = o_ref[...]
    x_ref, m_ref, w_ref, gf_ref, rw_ref, rb_ref, x1_ref, f_ref, ri_ref, cnt_ref, carry_ref = rest

    @pl.when(jnp.logical_and(b == 0, t == 0))
    def _():
        carry_ref[...] = jnp.zeros_like(carry_ref)

    x1 = x_ref[...] + _mod_row(m_ref, row, 2, d) * _dot(o, w_ref[...])
    x1_ref[...] = x1
    f = _prenorm_mod(x1, gf_ref[...], _mod_row(m_ref, row, 3, d), _mod_row(m_ref, row, 4, d))
    f_ref[...] = f
    logits = _dot_hi(f, rw_ref[...]) + rb_ref[...]
    lane = lax.broadcasted_iota(jnp.int32, logits.shape, 1)
    vals, idxs, sels = [], [], []
    cur = logits
    for _ in range(TOP_K):
        m = jnp.max(cur, axis=-1, keepdims=True)
        idx = jnp.min(jnp.where(cur == m, lane, LANES), axis=-1, keepdims=True)
        sel = lane == idx
        cur = jnp.where(sel, -jnp.inf, cur)
        vals.append(m)
        idxs.append(idx)
        sels.append(sel)
    es = [jnp.exp(v - vals[0]) for v in vals]
    esum = es[0] + es[1] + es[2] + es[3]
    assign = jnp.where(sels[0] | sels[1] | sels[2] | sels[3], 1.0, 0.0)
    ii = lax.broadcasted_iota(jnp.int32, (TM, TM), 0)
    jj = lax.broadcasted_iota(jnp.int32, (TM, TM), 1)
    before = _dot(jnp.where(ii > jj, 1.0, 0.0), assign) + carry_ref[...]
    info = jnp.zeros(logits.shape, F32)
    for k in range(TOP_K):
        rank = jnp.sum(jnp.where(sels[k], before, 0.0), axis=-1, keepdims=True)
        info = jnp.where(lane == k, idxs[k].astype(F32), info)
        info = jnp.where(lane == TOP_K + k, rank, info)
        info = jnp.where(lane == 2 * TOP_K + k, es[k] / esum, info)
    ri_ref[...] = info
    carry_ref[...] = carry_ref[...] + jnp.sum(assign, axis=0, keepdims=True)
    cnt_ref[...] = carry_ref[...]


def _outproj(kind, mixer_out, xt, mods, w_out, gf, rw, rb, nct, out_scale=1.0):
    nb, t_all, d = xt.shape
    nt = t_all // TM
    tile = lambda w: pl.BlockSpec((None, TM, w), lambda b, t: (b, t, 0))
    full = lambda a: pl.BlockSpec(a.shape, lambda b, t: (0,) * a.ndim)
    if kind in (0, 1):
        of, ob, pm, zoff, hg = mixer_out
        zblk = zoff // d
        lead = [of, ob, pm, hg]
        lead_specs = [tile(d), tile(d), pl.BlockSpec((None, TM, d), lambda b, t: (b, t, zblk)), full(hg)]
    elif kind == 2:
        o, hg = mixer_out
        lead = [o, hg]
        lead_specs = [tile(d), full(hg)]
    else:
        lead = [mixer_out]
        lead_specs = [tile(d)]
    args = lead + [xt, mods, w_out, gf, rw, rb]
    in_specs = lead_specs + [tile(d), full(mods), full(w_out), full(gf), full(rw), full(rb)]
    return pl.pallas_call(
        functools.partial(_outproj_kernel, kind, nct, nb, d, out_scale),
        out_shape=(jax.ShapeDtypeStruct((nb, t_all, d), F32), jax.ShapeDtypeStruct((nb, t_all, d), F32),
                   jax.ShapeDtypeStruct((nb, t_all, LANES), F32), jax.ShapeDtypeStruct((1, LANES), F32)),
        grid=(nb, nt), in_specs=in_specs,
        out_specs=(tile(d), tile(d), tile(LANES), pl.BlockSpec((1, LANES), lambda b, t: (0, 0))),
        scratch_shapes=[pltpu.VMEM((1, LANES), F32)],
        compiler_params=_cp(("arbitrary", "arbitrary")), name=f"outproj{kind}",
    )(*args)


def _dispatch_kernel(pos_ref, f_ref, xs_in, xs_out, sem):
    del xs_in

    def row_copy(r, k):
        return pltpu.make_async_copy(f_ref.at[pl.ds(r, 1)], xs_out.at[pl.ds(pos_ref[TOP_K * r + k], 1)], sem)

    def start(r, c):
        for k in range(TOP_K):
            row_copy(r, k).start()
        return c

    def wait(r, c):
        for k in range(TOP_K):
            row_copy(r, k).wait()
        return c

    lax.fori_loop(0, TM, start, 0)
    lax.fori_loop(0, TM, wait, 0)


def _dispatch(f2, pos_flat, n_rows):
    n, d = f2.shape
    return pl.pallas_call(
        _dispatch_kernel,
        out_shape=jax.ShapeDtypeStruct((n_rows, d), F32),
        grid=(n // TM,),
        in_specs=[pl.BlockSpec((TOP_K * TM,), lambda i: (i,), memory_space=pltpu.SMEM),
                  pl.BlockSpec((TM, d), lambda i: (i, 0)), pl.BlockSpec(memory_space=pl.ANY)],
        out_specs=pl.BlockSpec(memory_space=pl.ANY),
        scratch_shapes=[pltpu.SemaphoreType.DMA],
        input_output_aliases={2: 0},
        compiler_params=_cp(("arbitrary",)), name="moe_dispatch",
    )(pos_flat, f2, jnp.zeros((n_rows, d), F32))


def _expert_kernel(be_ref, xs_ref, wgu_ref, bgu_ref, wd_ref, bd_ref, ys_ref):
    del be_ref
    ff = wd_ref.shape[0]
    gu = _dot(xs_ref[...], wgu_ref[...]) + bgu_ref[...]
    x_glu = jnp.minimum(gu[:, :ff], SWIGLU_LIMIT)
    x_lin = jnp.clip(gu[:, ff:], -SWIGLU_LIMIT, SWIGLU_LIMIT)
    act = x_glu * _sigmoid(SWIGLU_ALPHA * x_glu) * (x_lin + 1.0)
    ys_ref[...] = _dot(act, wd_ref[...]) + bd_ref[...]


def _expert_ffn(xs, blk_exp, w_gu, b_gu, w_down, b_down):
    n_rows, d = xs.shape
    ne, _, f2 = w_gu.shape
    ff = w_down.shape[1]
    grid_spec = pltpu.PrefetchScalarGridSpec(
        num_scalar_prefetch=1, grid=(n_rows // MOE_BM,),
        in_specs=[pl.BlockSpec((MOE_BM, d), lambda i, be: (i, 0)),
                  pl.BlockSpec((None, d, f2), lambda i, be: (be[i], 0, 0)),
                  pl.BlockSpec((None, 1, f2), lambda i, be: (be[i], 0, 0)),
                  pl.BlockSpec((None, ff, d), lambda i, be: (be[i], 0, 0)),
                  pl.BlockSpec((None, 1, d), lambda i, be: (be[i], 0, 0))],
        out_specs=pl.BlockSpec((MOE_BM, d), lambda i, be: (i, 0)))
    return pl.pallas_call(
        _expert_kernel, out_shape=jax.ShapeDtypeStruct((n_rows, d), F32), grid_spec=grid_spec,
        compiler_params=_cp(("arbitrary",)), name="moe_experts",
    )(blk_exp, xs, w_gu, b_gu.reshape(ne, 1, f2), w_down, b_down.reshape(ne, 1, d))


def _combine_kernel(nct, nb, nt, d, pos_ref, ri_ref, x1_ref, m_ref, ys_hbm, x2_ref, buf, sem):
    i = pl.program_id(0)
    b = i // nt
    t = i % nt
    row = jnp.where(t < nct, nb, b)

    def row_copy(r, k):
        return pltpu.make_async_copy(ys_hbm.at[pl.ds(pos_ref[TOP_K * r + k], 1)], buf.at[k, pl.ds(r, 1)], sem)

    def start(r, c):
        for k in range(TOP_K):
            row_copy(r, k).start()
        return c

    def wait(r, c):
        for k in range(TOP_K):
            row_copy(r, k).wait()
        return c

    lax.fori_loop(0, TM, start, 0)
    lax.fori_loop(0, TM, wait, 0)
    ri = ri_ref[...]
    y = None
    for k in range(TOP_K):
        term = buf[k] * ri[:, 2 * TOP_K + k:2 * TOP_K + k + 1]
        y = term if y is None else y + term
    x2_ref[...] = x1_ref[...] + _mod_row(m_ref, row, 5, d) * y


def _combine(ys, pos_flat, rinfo2, x1_2, mods, nct, nb):
    n, d = x1_2.shape
    nt = n // TM // nb
    return pl.pallas_call(
        functools.partial(_combine_kernel, nct, nb, nt, d),
        out_shape=jax.ShapeDtypeStruct((n, d), F32),
        grid=(n // TM,),
        in_specs=[pl.BlockSpec((TOP_K * TM,), lambda i: (i,), memory_space=pltpu.SMEM),
                  pl.BlockSpec((TM, LANES), lambda i: (i, 0)),
                  pl.BlockSpec((TM, d), lambda i: (i, 0)),
                  pl.BlockSpec(mods.shape, lambda i: (0, 0)),
                  pl.BlockSpec(memory_space=pl.ANY)],
        out_specs=pl.BlockSpec((TM, d), lambda i: (i, 0)),
        scratch_shapes=[pltpu.VMEM((TOP_K, TM, d), F32), pltpu.SemaphoreType.DMA],
        compiler_params=_cp(("arbitrary",)), name="moe_combine",
    )(pos_flat, rinfo2, x1_2, mods, ys)


def _moe(f, rinfo, counts, x1, mods, w_gu, b_gu, w_down, b_down, nct):
    nb, t_all, d = f.shape
    n = nb * t_all
    ne = w_gu.shape[0]
    ri2 = rinfo.reshape(n, LANES)
    idx = ri2[:, :TOP_K].astype(jnp.int32)
    rank = ri2[:, TOP_K:2 * TOP_K].astype(jnp.int32)
    cnt = counts[0, :ne].astype(jnp.int32)
    padded = (cnt + MOE_BM - 1) // MOE_BM * MOE_BM
    pad_end = jnp.cumsum(padded)
    pad_start = pad_end - padded
    onehot = idx[:, :, None] == jnp.arange(ne, dtype=jnp.int32)[None, None, :]
    pos = (rank + jnp.sum(jnp.where(onehot, pad_start[None, None, :], 0), axis=-1)).reshape(-1)
    n_blk = (n * TOP_K + MOE_BM - 1) // MOE_BM + ne
    blk_start = jnp.arange(n_blk, dtype=jnp.int32) * MOE_BM
    blk_exp = jnp.minimum(jnp.sum(blk_start[:, None] >= pad_end[None, :], axis=-1), ne - 1).astype(jnp.int32)
    xs = _dispatch(f.reshape(n, d), pos, n_blk * MOE_BM)
    ys = _expert_ffn(xs, blk_exp, w_gu, b_gu, w_down, b_down)
    x2 = _combine(ys, pos, ri2, x1.reshape(n, d), mods, nct, nb)
    return x2.reshape(nb, t_all, d)


def _final_kernel(x_ref, g_ref, o_ref):
    x = x_ref[...]
    o_ref[...] = x * lax.rsqrt(jnp.mean(x * x, axis=-1, keepdims=True) + NORM_EPS) * g_ref[...]


def _final_norm(xt, g, nct):
    nb, t_all, d = xt.shape
    seq = t_all - nct * TM
    return pl.pallas_call(
        _final_kernel, out_shape=jax.ShapeDtypeStruct((nb, seq, d), F32), grid=(nb, seq // TM),
        in_specs=[pl.BlockSpec((None, TM, d), lambda b, t: (b, t + nct, 0)), pl.BlockSpec(g.shape, lambda b, t: (0, 0))],
        out_specs=pl.BlockSpec((None, TM, d), lambda b, t: (b, t, 0)),
        compiler_params=_cp(("arbitrary", "arbitrary")), name="final_norm",
    )(xt, g)


def _rope_tables(seq, lc):
    rows = seq // GRID_W
    row = jnp.repeat(jnp.arange(rows), GRID_W).astype(F32)
    col = jnp.tile(jnp.arange(GRID_W), rows).astype(F32)
    inv = ROPE_BASE ** (-jnp.arange(16, dtype=F32) / 16)
    ang_r = row[:, None] * inv[None, :]
    ang_c = col[:, None] * inv[None, :]
    cos = jnp.concatenate([jnp.cos(ang_r)] * 2 + [jnp.cos(ang_c)] * 2, axis=1)
    sin = jnp.concatenate([-jnp.sin(ang_r), jnp.sin(ang_r), -jnp.sin(ang_c), jnp.sin(ang_c)], axis=1)
    cos = jnp.concatenate([jnp.ones((lc, 64), F32), cos], axis=0)
    sin = jnp.concatenate([jnp.zeros((lc, 64), F32), sin], axis=0)
    return jnp.tile(cos, (1, 2)), jnp.tile(sin, (1, 2))


def _pad_lanes(a, fill=0.0):
    return jnp.pad(a, [(0, 0)] * (a.ndim - 1) + [(0, LANES - a.shape[-1])], constant_values=fill)


def _gate_consts(first, second):
    return _pad_lanes(jnp.concatenate([first.reshape(1, -1), second.reshape(1, -1)], axis=1))


def kernel(x, c, ctx, c_ctx, ada_w, ada_b, norm_mix_g, norm_ffn_g, final_g, gdn_w_in, gdn_conv_w, gdn_a_log, gdn_dt_bias, gdn_norm_g, gdn_w_out, mlstm_w_in, mlstm_b_i, mlstm_b_f, mlstm_norm_g, mlstm_w_out, diff_w_in, diff_lambda, diff_norm_g, diff_w_out, swa_w_in, swa_sink, swa_w_out, router_w, router_b, moe_w_gu, moe_b_gu, moe_w_down, moe_b_down):
    nb, seq, d = x.shape
    lc = ctx.shape[1]
    depth = ada_w.shape[0]
    assert lc % TM == 0 and seq % TM == 0 and nb < 8
    nct = lc // TM
    xt = jnp.concatenate([ctx, x], axis=1)
    cond8 = jnp.zeros((8, d), F32).at[:nb].set(c).at[nb].set(c_ctx)
    mods_all = _ada_mods(cond8, ada_w, ada_b)
    cos, sin = _rope_tables(seq, lc)
    for i in range(depth):
        kind, j = i % 4, i // 4
        mods = mods_all[i]
        gm = norm_mix_g[i].reshape(1, d)
        gf = norm_ffn_g[i].reshape(1, d)
        rw = _pad_lanes(router_w[i])
        rb = _pad_lanes(router_b[i].reshape(1, -1), -jnp.inf)
        if kind == 0:
            w = gdn_w_in[j]
            main = w.shape[1] - 32
            p, gt, gc = _inproj(0, xt, gm, mods, w[:, :main].astype(BF16),
                                (_pad_lanes(w[:, main:]), _gate_consts(gdn_a_log[j], jnp.zeros_like(gdn_a_log[j])),
                                 _gate_consts(gdn_dt_bias[j], jnp.zeros_like(gdn_dt_bias[j]))), nct)
            q, k, v = _gdn_prep(p, jnp.pad(gdn_conv_w[j], ((0, 3), (0, 0))), nct)
            of, ob = _gdn_scan(q, k, v, gt, gc, nct)
            mixer_out = (of, ob, p, 3 * d, gdn_norm_g[j].reshape(1, -1))
            w_out, scale = gdn_w_out[j], 1.0
        elif kind == 1:
            w = mlstm_w_in[j]
            main = w.shape[1] - 32
            p, gt, gc = _inproj(1, xt, gm, mods, w[:, :main].astype(BF16),
                                (_pad_lanes(w[:, main:]), _gate_consts(mlstm_b_i[j], jnp.zeros_like(mlstm_b_i[j])),
                                 _pad_lanes(jnp.concatenate([jnp.zeros((1, 16), F32), mlstm_b_f[j].reshape(1, -1)], axis=1))),
                                nct)
            of, ob = _mlstm_scan(p[..., :d // 2], p[..., d // 2:d], p[..., d:2 * d], gt, gc, nct)
            mixer_out = (of, ob, p, 2 * d, mlstm_norm_g[j].reshape(1, -1))
            w_out, scale = mlstm_w_out[j], 1.0
        elif kind == 2:
            lambda_init = 0.8 - 0.6 * math.exp(-0.3 * i)
            q, k, v = _inproj(2, xt, gm, mods, diff_w_in[j].astype(BF16), (cos, sin, (d, d, d)), nct)
            o = jnp.concatenate([_diff_attn(q[:, :lc], k[:, :lc], v[:, :lc], diff_lambda[j], lambda_init),
                                 _diff_attn(q[:, lc:], k, v, diff_lambda[j], lambda_init)], axis=1)
            mixer_out = (o, diff_norm_g[j].reshape(1, -1))
            w_out, scale = diff_w_out[j], 1.0 - lambda_init
        else:
            q, k, v = _inproj(3, xt, gm, mods, swa_w_in[j].astype(BF16), (cos, sin, (d, d // 4, d // 4)), nct)
            mixer_out = _swa_attn(q, k, v, _pad_lanes(swa_sink[j].reshape(1, -1)), nct, seq)
            w_out, scale = swa_w_out[j], 1.0
        x1, f, rinfo, counts = _outproj(kind, mixer_out, xt, mods, w_out.astype(BF16), gf, rw, rb, nct, scale)
        xt = _moe(f, rinfo, counts, x1, mods, moe_w_gu[i].astype(BF16), moe_b_gu[i], moe_w_down[i].astype(BF16),
                  moe_b_down[i], nct)
    return _final_norm(xt, final_g.reshape(1, d), nct)
```

```python
import functools
import math

import jax
import jax.numpy as jnp
from jax import lax
from jax.experimental import pallas as pl
from jax.experimental.pallas import tpu as pltpu

F32 = jnp.float32
BF16 = jnp.bfloat16
HI = lax.Precision.HIGHEST

NORM_EPS = 1e-6
ROPE_BASE = 10000.0
GRID_W = 64
TM = 256
CHUNK = 64
CHUNK_SHIFT = 6
GDN_HEADS_PER_STEP = 2
Q_SCALE = {2: 0.125 * math.log2(math.e), 3: 0.125}
LANES = 128
N_EXPERTS = 32
TOP_K = 4
SWIGLU_ALPHA = 1.702
SWIGLU_LIMIT = 7.0
MOE_BM = 256
SWA_WINDOW = 128
VMEM_LIMIT = 56 * 1024 * 1024


def _cp(sem, vmem=VMEM_LIMIT):
    return pltpu.CompilerParams(dimension_semantics=sem, vmem_limit_bytes=vmem)


def _dot(a, b):
    return jnp.dot(a.astype(BF16), b.astype(BF16), preferred_element_type=F32)


def _dot_nt(a, b):
    return lax.dot_general(a.astype(BF16), b.astype(BF16), (((1,), (1,)), ((), ())),
                           preferred_element_type=F32)


def _dot_tn(a, b):
    return lax.dot_general(a.astype(BF16), b.astype(BF16), (((0,), (0,)), ((), ())),
                           preferred_element_type=F32)


def _dot_hi(a, b):
    return jnp.dot(a, b, precision=HI, preferred_element_type=F32)


def _sigmoid(x):
    return 1.0 / (1.0 + jnp.exp(-x))


def _softplus(x):
    return jnp.maximum(x, 0.0) + jnp.log(1.0 + jnp.exp(-jnp.abs(x)))


def _mods_kernel(c_ref, w_ref, b_ref, o_ref):
    c = c_ref[...]
    o_ref[...] = _dot_hi(c * _sigmoid(c), w_ref[...]) + b_ref[...]


def _ada_mods(cond8, ada_w, ada_b):
    depth, d, d6 = ada_w.shape
    return pl.pallas_call(
        _mods_kernel,
        out_shape=jax.ShapeDtypeStruct((depth, 8, d6), F32),
        grid=(depth, d6 // d),
        in_specs=[pl.BlockSpec((8, d), lambda i, j: (0, 0)),
                  pl.BlockSpec((None, d, d), lambda i, j: (i, 0, j)),
                  pl.BlockSpec((None, 1, d), lambda i, j: (i, 0, j))],
        out_specs=pl.BlockSpec((None, 8, d), lambda i, j: (i, 0, j)),
        compiler_params=_cp(("arbitrary", "arbitrary")),
        name="ada_mods",
    )(cond8, ada_w, ada_b.reshape(depth, 1, d6))


def _mod_row(m_ref, row, k, d):
    return m_ref[pl.ds(row, 1), k * d:(k + 1) * d]


def _prenorm_mod(x, g, shift, scale):
    ms = jnp.mean(x * x, axis=-1, keepdims=True)
    y = x * lax.rsqrt(ms + NORM_EPS) * g
    return y * (1.0 + scale) + shift


def _rope_slab(x, cos, sin, first_half):
    part = jnp.where(first_half, pltpu.roll(x, LANES - 16, 1), pltpu.roll(x, 16, 1))
    return x * cos + part * sin


def _inproj_kernel(kind, nct, nb, d, *refs):
    b = pl.program_id(0)
    t = pl.program_id(1)
    row = jnp.where(t < nct, nb, b)
    x_ref, g_ref, m_ref, w_ref = refs[:4]
    rest = refs[4:]
    h = _prenorm_mod(x_ref[...], g_ref[...], _mod_row(m_ref, row, 0, d), _mod_row(m_ref, row, 1, d))
    p = _dot(h, w_ref[...])
    if kind in (0, 1):
        wg_ref, c0_ref, c1_ref, p_ref, gt_ref, gc_ref = rest
        p_ref[...] = p
        graw = _dot_hi(h, wg_ref[...])
        lane = lax.broadcasted_iota(jnp.int32, graw.shape, 1)
        if kind == 0:
            gate = jnp.where(lane < 16, -jnp.exp(c0_ref[...]) * _softplus(graw + c1_ref[...]),
                             _sigmoid(graw))
        else:
            gate = jnp.where(lane < 16, graw + c0_ref[...], -_softplus(-(graw + c1_ref[...])))
        ii = lax.broadcasted_iota(jnp.int32, (TM, TM), 0)
        jj = lax.broadcasted_iota(jnp.int32, (TM, TM), 1)
        same = jnp.right_shift(ii, CHUNK_SHIFT) == jnp.right_shift(jj, CHUNK_SHIFT)
        cum_f = _dot_hi(jnp.where(same, jnp.where(jj <= ii, 1.0, 0.0), 0.0), gate)
        cum_b = _dot_hi(jnp.where(same, jnp.where(jj >= ii, 1.0, 0.0), 0.0), gate)
        cum = jnp.where(lane % 16 < 8, cum_f, cum_b)
        decay_col = jnp.right_shift(lane, 4) == kind
        gate = jnp.where(decay_col, cum, gate)
        gt_ref[...] = gate
        gc_ref[...] = gate.T
    else:
        cos_ref, sin_ref, q_ref, k_ref, v_ref = rest
        cos = cos_ref[...]
        sin = sin_ref[...]
        lane = lax.broadcasted_iota(jnp.int32, cos.shape, 1)
        first_half = (lane % 32) < 16
        nq = q_ref.shape[-1] // LANES
        nk = k_ref.shape[-1] // LANES
        for s in range(nq):
            q_ref[:, s * LANES:(s + 1) * LANES] = (
                _rope_slab(p[:, s * LANES:(s + 1) * LANES], cos, sin, first_half) * Q_SCALE[kind]).astype(BF16)
        for s in range(nk):
            o = (nq + s) * LANES
            k_ref[:, s * LANES:(s + 1) * LANES] = _rope_slab(p[:, o:o + LANES], cos, sin, first_half).astype(BF16)
        o = (nq + nk) * LANES
        v_ref[...] = p[:, o:].astype(BF16)


def _inproj(kind, xt, g, mods, w_main, extra, nct):
    nb, t_all, d = xt.shape
    nt = t_all // TM
    pdim = w_main.shape[1]
    tile = lambda w: pl.BlockSpec((None, TM, w), lambda b, t: (b, t, 0))
    full = lambda a: pl.BlockSpec(a.shape, lambda b, t: (0,) * a.ndim)
    in_specs = [tile(d), full(g), full(mods), full(w_main)]
    if kind in (0, 1):
        wg, c0, c1 = extra
        in_specs += [full(wg), full(c0), full(c1)]
        out_shape = (jax.ShapeDtypeStruct((nb, t_all, pdim), F32),
                     jax.ShapeDtypeStruct((nb, t_all, LANES), F32),
                     jax.ShapeDtypeStruct((nb, LANES, t_all), F32))
        out_specs = (tile(pdim), tile(LANES), pl.BlockSpec((None, LANES, TM), lambda b, t: (b, 0, t)))
        args = (xt, g, mods, w_main, wg, c0, c1)
    else:
        cos, sin, widths = extra
        in_specs += [pl.BlockSpec((TM, LANES), lambda b, t: (t, 0))] * 2
        out_shape = tuple(jax.ShapeDtypeStruct((nb, t_all, w), BF16) for w in widths)
        out_specs = tuple(tile(w) for w in widths)
        args = (xt, g, mods, w_main, cos, sin)
    return pl.pallas_call(
        functools.partial(_inproj_kernel, kind, nct, nb, d),
        out_shape=out_shape, grid=(nb, nt), in_specs=in_specs, out_specs=out_specs,
        compiler_params=_cp(("arbitrary", "arbitrary")), name=f"inproj{kind}",
    )(*args)


def _gdn_prep_kernel(nct, nt, p_ref, prev_ref, next_ref, w_ref, q_ref, k_ref, v_ref, win_ref):
    t = pl.program_id(1)
    prev_ok = jnp.logical_and(t != 0, t != nct)
    next_ok = jnp.logical_and(t != nct - 1, t != nt - 1)
    win_ref[0:8, :] = jnp.where(prev_ok, prev_ref[...], 0.0)
    win_ref[8:8 + TM, :] = p_ref[...]
    win_ref[8 + TM:, :] = jnp.where(next_ok, next_ref[...], 0.0)
    width = q_ref.shape[-1]
    for part, o_ref in enumerate((q_ref, k_ref, v_ref)):
        for h in range(width // LANES):
            c0 = part * width + h * LANES
            acc = None
            for j in range(5):
                term = win_ref[6 + j:6 + j + TM, c0:c0 + LANES] * w_ref[j:j + 1, c0:c0 + LANES]
                acc = term if acc is None else acc + term
            y = acc * _sigmoid(acc)
            if part < 2:
                y = y * lax.rsqrt(jnp.sum(y * y, axis=-1, keepdims=True) + NORM_EPS)
            if part == 0:
                y = y * (LANES ** -0.5)
            o_ref[:, h * LANES:(h + 1) * LANES] = y


def _gdn_prep(p_main, conv_w8, nct):
    nb, t_all, _ = p_main.shape
    nt = t_all // TM
    width = conv_w8.shape[1] // 3
    cw = 3 * width
    t8 = t_all // 8
    r8 = TM // 8
    tile = pl.BlockSpec((None, TM, width), lambda b, t: (b, t, 0))
    return pl.pallas_call(
        functools.partial(_gdn_prep_kernel, nct, nt),
        out_shape=tuple(jax.ShapeDtypeStruct((nb, t_all, width), F32) for _ in range(3)),
        grid=(nb, nt),
        in_specs=[pl.BlockSpec((None, TM, cw), lambda b, t: (b, t, 0)),
                  pl.BlockSpec((None, 8, cw), lambda b, t: (b, jnp.maximum(t * r8 - 1, 0), 0)),
                  pl.BlockSpec((None, 8, cw), lambda b, t: (b, jnp.minimum((t + 1) * r8, t8 - 1), 0)),
                  pl.BlockSpec(conv_w8.shape, lambda b, t: (0, 0))],
        out_specs=(tile, tile, tile),
        scratch_shapes=[pltpu.VMEM((TM + 16, cw), F32)],
        compiler_params=_cp(("arbitrary", "arbitrary")), name="gdn_prep",
    )(p_main, p_main, p_main, conv_w8)


def _split_bf16(a):
    hi = a.astype(BF16)
    return hi, (a - hi.astype(F32)).astype(BF16)


def _dot3(a, b):
    ah, al = _split_bf16(a)
    bh, bl = _split_bf16(b)
    d = lambda x, y: jnp.dot(x, y, preferred_element_type=F32)
    return d(ah, bh) + (d(ah, bl) + d(al, bh))


def _unit_triangular_inverses(mats, same):
    a8 = [jnp.where(same(3), a, 0.0) for a in mats]
    ts = [jnp.where(same(0), 1.0, 0.0) - x for x in a8]
    p2 = [_dot3(x, x) for x in a8]
    ts = [t + _dot3(t, p) for t, p in zip(ts, p2)]
    p4 = [_dot3(p, p) for p in p2]
    ts = [t + _dot3(t, p) for t, p in zip(ts, p4)]
    for sh in (4, 5, 6):
        offs = [jnp.where(same(sh), jnp.where(same(sh - 1), 0.0, a), 0.0) for a in mats]
        mids = [_dot3(o, t) for o, t in zip(offs, ts)]
        ts = [t - _dot3(t, m) for t, m in zip(ts, mids)]
    return ts


def _gdn_tiles(probs):
    ii = lax.broadcasted_iota(jnp.int32, (TM, TM), 0)
    jj = lax.broadcasted_iota(jnp.int32, (TM, TM), 1)
    same = lambda sh: jnp.right_shift(ii, sh) == jnp.right_shift(jj, sh)
    incl = {rev: jnp.where(same(CHUNK_SHIFT), jnp.where((ii <= jj) if rev else (ii >= jj), 1.0, 0.0), 0.0) > 0.5
            for rev in {p[8] for p in probs}}
    gams = [jnp.where(incl[rev], jnp.exp(jnp.where(incl[rev], gc - gr, 0.0)), 0.0)
            for (_, _, _, _, gc, gr, _, _, rev) in probs]
    kbs = [p[1] * p[3] for p in probs]
    mats = [jnp.where(same(0), 0.0, _dot_nt(kb, p[1]) * gam) for kb, p, gam in zip(kbs, probs, gams)]
    tinvs = _unit_triangular_inverses(mats, same)
    egs = [jnp.exp(p[4]) for p in probs]
    uws = [_dot3(t, jnp.concatenate([p[2] * p[3], kb * eg], axis=1)) for t, p, kb, eg in zip(tinvs, probs, kbs, egs)]
    qks = [_dot_nt(p[0], p[1]) * gam for p, gam in zip(probs, gams)]
    qgs = [p[0] * eg for p, eg in zip(probs, egs)]
    states = [p[6] for p in probs]
    nchunk = TM // CHUNK
    for n in range(nchunk):
        cs = [nchunk - 1 - n if p[8] else n for p in probs]
        sls = [slice(c * CHUNK, (c + 1) * CHUNK) for c in cs]
        gtots = [p[4][(c * CHUNK if p[8] else (c + 1) * CHUNK - 1):][:1] for p, c in zip(probs, cs)]
        wss = [_dot(uw[sl, LANES:], s) for uw, sl, s in zip(uws, sls, states)]
        qss = [_dot(qg[sl], s) for qg, sl, s in zip(qgs, sls, states)]
        vns = [uw[sl, :LANES] - ws for uw, sl, ws in zip(uws, sls, wss)]
        for p, sl, qs, qk, vn in zip(probs, sls, qss, qks, vns):
            p[7](sl, qs + _dot(qk[sl, sl], vn))
        states = [s * jnp.exp(gt) + _dot_tn(p[1][sl] * jnp.exp(gt - p[4][sl]), vn)
                  for s, gt, p, sl, vn in zip(states, gtots, probs, sls, vns)]
    return states


def _gate_col(gt, idx):
    lane = lax.broadcasted_iota(jnp.int32, gt.shape, 1)
    return jnp.sum(jnp.where(lane == idx, gt, 0.0), axis=1, keepdims=True)


def _gate_row(gc, idx):
    sub = lax.broadcasted_iota(jnp.int32, gc.shape, 0)
    return jnp.sum(jnp.where(sub == idx, gc, 0.0), axis=0, keepdims=True)


def _gdn_scan_kernel(qf, kf, vf, gtf, gcf, qb, kb, vb, gtb, gcb, of_ref, ob_ref, s_ref):
    hp = pl.program_id(1)

    @pl.when(pl.program_id(2) == 0)
    def _():
        s_ref[...] = jnp.zeros_like(s_ref)

    probs = []
    for rev, (q_r, k_r, v_r, gt_r, gc_r, o_r) in enumerate(
            ((qf, kf, vf, gtf, gcf, of_ref), (qb, kb, vb, gtb, gcb, ob_ref))):
        gt = gt_r[...]
        gc = gc_r[...]
        for j in range(GDN_HEADS_PER_STEP):
            head = hp * GDN_HEADS_PER_STEP + j
            cols = slice(j * LANES, (j + 1) * LANES)

            def store(sl, val, o_r=o_r, cols=cols):
                o_r[sl, cols] = val

            probs.append((q_r[:, cols], k_r[:, cols], v_r[:, cols], _gate_col(gt, 16 + rev * 8 + head),
                          _gate_col(gt, rev * 8 + head), _gate_row(gc, rev * 8 + head), s_ref[rev, j], store,
                          bool(rev)))
    states = _gdn_tiles(probs)
    for rev in range(2):
        for j in range(GDN_HEADS_PER_STEP):
            s_ref[rev, j] = states[rev * GDN_HEADS_PER_STEP + j]


def _scan_tiles(nct, nt):
    fwd = lambda s: s
    bwd = lambda s: jnp.where(s < nct, nct - 1 - s, nt - 1 - (s - nct))
    return fwd, bwd


def _gdn_scan(q, k, v, gt, gc, nct):
    nb, t_all, width = q.shape
    hw = GDN_HEADS_PER_STEP * LANES
    nt = t_all // TM
    fwd, bwd = _scan_tiles(nct, nt)

    def specs(tile_of):
        head = pl.BlockSpec((None, TM, hw), lambda b, h, s: (b, tile_of(s), h))
        return [head, head, head,
                pl.BlockSpec((None, TM, LANES), lambda b, h, s: (b, tile_of(s), 0)),
                pl.BlockSpec((None, LANES, TM), lambda b, h, s: (b, 0, tile_of(s)))]

    out = jax.ShapeDtypeStruct((nb, t_all, width), F32)
    return pl.pallas_call(
        _gdn_scan_kernel, out_shape=(out, out), grid=(nb, width // hw, nt),
        in_specs=specs(fwd) + specs(bwd),
        out_specs=(pl.BlockSpec((None, TM, hw), lambda b, h, s: (b, fwd(s), h)),
                   pl.BlockSpec((None, TM, hw), lambda b, h, s: (b, bwd(s), h))),
        scratch_shapes=[pltpu.VMEM((2, GDN_HEADS_PER_STEP, LANES, LANES), F32)],
        compiler_params=_cp(("arbitrary", "arbitrary", "arbitrary")), name="gdn_scan",
    )(q, k, v, gt, gc, q, k, v, gt, gc)


def _mlstm_tiles(probs):
    ii = lax.broadcasted_iota(jnp.int32, (TM, TM), 0)
    jj = lax.broadcasted_iota(jnp.int32, (TM, TM), 1)
    same_chunk = jnp.right_shift(ii, CHUNK_SHIFT) == jnp.right_shift(jj, CHUNK_SHIFT)
    incl = {rev: jnp.where(same_chunk, jnp.where((ii <= jj) if rev else (ii >= jj), 1.0, 0.0), 0.0) > 0.5
            for rev in {p[10] for p in probs}}
    nchunk = TM // CHUNK
    order = lambda p: [nchunk - 1 - n if p[10] else n for n in range(nchunk)]
    rows = lambda c: slice(c * CHUNK, (c + 1) * CHUNK)
    dlogs = [jnp.where(incl[p[10]], p[5] - p[6] + p[4], -jnp.inf) for p in probs]
    m_intras = [jnp.max(d, axis=-1, keepdims=True) for d in dlogs]
    qks = [_dot_nt(p[0], p[1]) for p in probs]
    btots, m_locs, c_locs = [], [], []
    for p in probs:
        bt, ml, cl = {}, {}, {}
        for c in order(p):
            last = c * CHUNK if p[10] else (c + 1) * CHUNK - 1
            bt[c] = p[5][last:last + 1]
            a_loc = bt[c] - p[5][rows(c)] + p[3][rows(c)]
            ml[c] = jnp.max(a_loc, axis=0, keepdims=True)
            cl[c] = _dot_tn(p[1][rows(c)] * jnp.exp(a_loc - ml[c]), p[2][rows(c)])
        btots.append(bt)
        m_locs.append(ml)
        c_locs.append(cl)
    new_states, c_ins, m_ins = [], [], []
    for p, bt, ml, cl in zip(probs, btots, m_locs, c_locs):
        c_st, m_st = p[7], p[8]
        ci, mi = {}, {}
        for c in order(p):
            ci[c], mi[c] = c_st, m_st
            m_new = jnp.maximum(bt[c] + m_st, ml[c])
            c_st = c_st * jnp.exp(bt[c] + m_st - m_new) + cl[c] * jnp.exp(ml[c] - m_new)
            m_st = m_new
        new_states.append((c_st, m_st))
        c_ins.append(ci)
        m_ins.append(mi)
    inters = [jnp.concatenate([_dot(p[0][rows(c)], ci[c]) for c in range(nchunk)], axis=0)
              for p, ci in zip(probs, c_ins)]
    for p, dlog, m_intra, qk, inter, mi in zip(probs, dlogs, m_intras, qks, inters, m_ins):
        m_in = jnp.concatenate([jnp.broadcast_to(mi[c], (CHUNK, 1)) for c in range(nchunk)], axis=0)
        log_inter = p[5] + m_in
        m_t = jnp.maximum(log_inter, m_intra)
        num = jnp.exp(log_inter - m_t) * inter + _dot(jnp.exp(dlog - m_t) * qk, p[2])
        dv = p[2].shape[1] // 2
        p[9](num[:, :dv] / jnp.maximum(jnp.abs(num[:, dv:dv + 1]), jnp.exp(-m_t)))
    return new_states


def _mlstm_scan_kernel(qf, kf, vf, gtf, gcf, qb, kb, vb, gtb, gcb, of_ref, ob_ref, c_ref, m_ref):
    hp = pl.program_id(1)

    @pl.when(pl.program_id(2) == 0)
    def _():
        c_ref[...] = jnp.zeros_like(c_ref)
        m_ref[...] = jnp.zeros_like(m_ref)

    dqk = qf.shape[-1] // 2
    dv = vf.shape[-1] // 2
    lane = lax.broadcasted_iota(jnp.int32, (TM, dv), 1)
    ones_pad = jnp.where(lane == 0, 1.0, 0.0)
    probs = []
    for rev, (q_r, k_r, v_r, gt_r, gc_r, o_r) in enumerate(
            ((qf, kf, vf, gtf, gcf, of_ref), (qb, kb, vb, gtb, gcb, ob_ref))):
        gt = gt_r[...]
        gc = gc_r[...]
        for j in range(2):
            head = 2 * hp + j

            def store(val, o_r=o_r, j=j):
                o_r[:, j * dv:(j + 1) * dv] = val

            probs.append((q_r[:, j * dqk:(j + 1) * dqk], k_r[:, j * dqk:(j + 1) * dqk] * (dqk ** -0.5),
                          jnp.concatenate([v_r[:, j * dv:(j + 1) * dv], ones_pad], axis=1),
                          _gate_col(gt, rev * 8 + head), _gate_row(gc, rev * 8 + head),
                          _gate_col(gt, 16 + rev * 8 + head), _gate_row(gc, 16 + rev * 8 + head),
                          c_ref[rev, j], m_ref[rev, j][0:1, 0:1], store, bool(rev)))
    states = _mlstm_tiles(probs)
    for rev in range(2):
        for j in range(2):
            c_st, m_st = states[rev * 2 + j]
            c_ref[rev, j] = c_st
            m_ref[rev, j] = jnp.broadcast_to(m_st, (8, LANES))


def _mlstm_scan(p, qw, vw, gt, gc, nct):
    nb, t_all, _ = p.shape
    nhp = qw // LANES
    nt = t_all // TM
    dqk = LANES // 2
    dv = vw // nhp // 2
    fwd, bwd = _scan_tiles(nct, nt)

    def specs(tile_of):
        return [pl.BlockSpec((None, TM, LANES), lambda b, h, s: (b, tile_of(s), h)),
                pl.BlockSpec((None, TM, LANES), lambda b, h, s: (b, tile_of(s), nhp + h)),
                pl.BlockSpec((None, TM, 2 * dv), lambda b, h, s: (b, tile_of(s), 2 * qw // (2 * dv) + h)),
                pl.BlockSpec((None, TM, LANES), lambda b, h, s: (b, tile_of(s), 0)),
                pl.BlockSpec((None, LANES, TM), lambda b, h, s: (b, 0, tile_of(s)))]

    out = jax.ShapeDtypeStruct((nb, t_all, vw), F32)
    return pl.pallas_call(
        _mlstm_scan_kernel, out_shape=(out, out), grid=(nb, nhp, nt),
        in_specs=specs(fwd) + specs(bwd),
        out_specs=(pl.BlockSpec((None, TM, 2 * dv), lambda b, h, s: (b, fwd(s), h)),
                   pl.BlockSpec((None, TM, 2 * dv), lambda b, h, s: (b, bwd(s), h))),
        scratch_shapes=[pltpu.VMEM((2, 2, dqk, 2 * dv), F32), pltpu.VMEM((2, 2, 8, LANES), F32)],
        compiler_params=_cp(("arbitrary", "arbitrary", "arbitrary")), name="mlstm_scan",
    )(p, p, p, gt, gc, p, p, p, gt, gc)


def _diff_attn_kernel(nk, lambda_init, lam_ref, q_ref, k_ref, v_ref, o_ref, m_ref, acc_ref):
    j = pl.program_id(3)
    dh = LANES // 2

    @pl.when(j == 0)
    def _():
        m_ref[...] = jnp.full(m_ref.shape, -jnp.inf, F32)
        acc_ref[...] = jnp.zeros_like(acc_ref)

    tq = q_ref.shape[0]
    tk = k_ref.shape[0]
    ones_col = jnp.where(lax.broadcasted_iota(jnp.int32, (tk, LANES), 1) == 0, 1.0, 0.0).astype(BF16)
    v_aug = jnp.concatenate([v_ref[...], ones_col], axis=1)
    nsplit = 2 if tq % 512 == 0 else 1
    rs = tq // nsplit
    probs = [(hh, slice(r * rs, (r + 1) * rs)) for r in range(nsplit) for hh in range(2)]
    scores = [lax.dot_general(q_ref[rows, hh * dh:(hh + 1) * dh], k_ref[:, hh * dh:(hh + 1) * dh],
                              (((1,), (1,)), ((), ())), preferred_element_type=F32) for hh, rows in probs]
    blocks = [slice(c * LANES, (c + 1) * LANES) for c in range(tk // LANES)]
    for (hh, rows), s in zip(probs, scores):
        part = s[:, blocks[0]]
        for blk in blocks[1:]:
            part = jnp.maximum(part, s[:, blk])
        m_old = m_ref[hh, rows, :]
        m_new = jnp.maximum(m_old, jnp.broadcast_to(jnp.max(part, axis=-1, keepdims=True), (rs, LANES)))
        alpha = jnp.exp2(m_old - m_new)
        p = jnp.concatenate([jnp.exp2(s[:, blk] - m_new).astype(BF16) for blk in blocks], axis=1)
        acc_ref[hh, rows, :] = (jnp.concatenate([alpha, alpha], axis=1) * acc_ref[hh, rows, :]
                                + jnp.dot(p, v_aug, preferred_element_type=F32))
        m_ref[hh, rows, :] = m_new

    @pl.when(j == nk - 1)
    def _():
        lp = lam_ref[...]
        lam = (jnp.exp(jnp.sum(lp[0:1] * lp[1:2], axis=-1, keepdims=True))
               - jnp.exp(jnp.sum(lp[2:3] * lp[3:4], axis=-1, keepdims=True)) + lambda_init)
        a0 = acc_ref[0]
        a1 = acc_ref[1]
        o_ref[...] = a0[:, :LANES] / a0[:, LANES:LANES + 1] - lam * (a1[:, :LANES] / a1[:, LANES:LANES + 1])


def _pick_tile(n, candidates):
    return next(c for c in candidates if n % c == 0)


def _diff_attn(q, k, v, lam_p, lambda_init):
    nb, nq_rows, width = q.shape
    ns = k.shape[1]
    tq = _pick_tile(nq_rows, (1024, 512, 256))
    tk = _pick_tile(ns, (1280, 1024, 768, 512, 256))
    nk = ns // tk
    kv = pl.BlockSpec((None, tk, LANES), lambda b, h, i, j: (b, j, h))
    qo = pl.BlockSpec((None, tq, LANES), lambda b, h, i, j: (b, i, h))
    return pl.pallas_call(
        functools.partial(_diff_attn_kernel, nk, lambda_init),
        out_shape=jax.ShapeDtypeStruct((nb, nq_rows, width), F32),
        grid=(nb, width // LANES, nq_rows // tq, nk),
        in_specs=[pl.BlockSpec(lam_p.shape, lambda b, h, i, j: (0, 0)), qo, kv, kv],
        out_specs=qo,
        scratch_shapes=[pltpu.VMEM((2, tq, LANES), F32), pltpu.VMEM((2, tq, 2 * LANES), F32)],
        compiler_params=_cp(("arbitrary", "arbitrary", "arbitrary", "arbitrary")), name="diff_attn",
    )(lam_p, q, k, v)


def _swa_kernel(nct, nt, seq, sink_ref, q_ref, kc_ref, kp_ref, k0_ref, kn_ref, vc_ref, vp_ref, v0_ref, vn_ref,
                o_ref):
    i = pl.program_id(1)
    dh = LANES // 2
    nkv = kc_ref.shape[-1] // dh
    group = (q_ref.shape[-1] // dh) // nkv
    lc = kc_ref.shape[0]
    col = lax.broadcasted_iota(jnp.int32, (TM, lc + 3 * TM), 1)
    qpos = (i - nct) * TM + lax.broadcasted_iota(jnp.int32, (TM, lc + 3 * TM), 0)
    kpos = (i - nct - 1) * TM + col - lc
    ninf = -jnp.inf
    in_window = jnp.where(jnp.abs(qpos - kpos) <= SWA_WINDOW,
                          jnp.where(kpos >= 0, jnp.where(kpos < seq, 0.0, ninf), ninf), ninf)
    bias = jnp.where(col < lc, 0.0, jnp.where(i >= nct, in_window, ninf))
    kcat = jnp.concatenate([kc_ref[...], kp_ref[...], k0_ref[...], kn_ref[...]], axis=0)
    vcat = jnp.concatenate([vc_ref[...], vp_ref[...], v0_ref[...], vn_ref[...]], axis=0)
    sink = sink_ref[...]
    scores = [lax.dot_general(q_ref[:, hq * dh:(hq + 1) * dh], kcat[:, (hq // group) * dh:(hq // group + 1) * dh],
                              (((1,), (1,)), ((), ())), preferred_element_type=F32) for hq in range(nkv * group)]
    for hq, s in enumerate(scores):
        g = hq // group
        s = s + bias
        sb = sink[0:1, hq:hq + 1]
        m = jnp.maximum(jnp.max(s, axis=-1, keepdims=True), sb)
        e = jnp.exp(s - m)
        p = e / (jnp.sum(e, axis=-1, keepdims=True) + jnp.exp(sb - m))
        o_ref[:, hq * dh:(hq + 1) * dh] = jnp.dot(p.astype(BF16), vcat[:, g * dh:(g + 1) * dh],
                                                  preferred_element_type=F32)


def _swa_attn(q, k, v, sink_row, nct, seq):
    nb, t_all, qw = q.shape
    kw = k.shape[-1]
    nt = t_all // TM
    lc = nct * TM
    ctxs = pl.BlockSpec((None, lc, kw), lambda b, i: (b, 0, 0))
    prev = pl.BlockSpec((None, TM, kw), lambda b, i: (b, jnp.maximum(i - 1, 0), 0))
    cur = pl.BlockSpec((None, TM, kw), lambda b, i: (b, i, 0))
    nxt = pl.BlockSpec((None, TM, kw), lambda b, i: (b, jnp.minimum(i + 1, nt - 1), 0))
    qo = pl.BlockSpec((None, TM, qw), lambda b, i: (b, i, 0))
    return pl.pallas_call(
        functools.partial(_swa_kernel, nct, nt, seq),
        out_shape=jax.ShapeDtypeStruct((nb, t_all, qw), F32),
        grid=(nb, nt),
        in_specs=[pl.BlockSpec(sink_row.shape, lambda b, i: (0, 0)), qo, ctxs, prev, cur, nxt, ctxs, prev, cur, nxt],
        out_specs=qo,
        compiler_params=_cp(("arbitrary", "arbitrary")), name="swa_attn",
    )(sink_row, q, k, k, k, k, v, v, v, v)


def _headnorm(o, g_row):
    outs = []
    for h in range(o.shape[-1] // LANES):
        seg = o[:, h * LANES:(h + 1) * LANES]
        ms = jnp.mean(seg * seg, axis=-1, keepdims=True)
        outs.append(seg * lax.rsqrt(ms + NORM_EPS) * g_row)
    return jnp.concatenate(outs, axis=1)


def _outproj_kernel(kind, nct, nb, d, out_scale, *refs):
    b = pl.program_id(0)
    t = pl.program_id(1)
    row = jnp.where(t < nct, nb, b)
    if kind in (0, 1):
        of_ref, ob_ref, z_ref, hg_ref = refs[:4]
        rest = refs[4:]
        z = z_ref[...]
        o = _headnorm(of_ref[...] + ob_ref[...], hg_ref[...]) * (z * _sigmoid(z) if kind == 0 else _sigmoid(z))
    elif kind == 2:
        o_ref, hg_ref = refs[:2]
        rest = refs[2:]
        o = _headnorm(o_ref[...], hg_ref[...]) * out_scale
    else:
        o_ref = refs[0]
        rest = refs[1:]
        o = o_ref[...]
    x_ref, m_ref, w_ref, gf_ref, rw_ref, rb_ref, x1_ref, f_ref, ri_ref, cnt_ref, carry_ref = rest

    @pl.when(jnp.logical_and(b == 0, t == 0))
    def _():
        carry_ref[...] = jnp.zeros_like(carry_ref)

    x1 = x_ref[...] + _mod_row(m_ref, row, 2, d) * _dot(o, w_ref[...])
    x1_ref[...] = x1
    f = _prenorm_mod(x1, gf_ref[...], _mod_row(m_ref, row, 3, d), _mod_row(m_ref, row, 4, d))
    f_ref[...] = f
    logits = _dot_hi(f, rw_ref[...]) + rb_ref[...]
    lane = lax.broadcasted_iota(jnp.int32, logits.shape, 1)
    vals, idxs, sels = [], [], []
    cur = logits
    for _ in range(TOP_K):
        m = jnp.max(cur, axis=-1, keepdims=True)
        idx = jnp.min(jnp.where(cur == m, lane, LANES), axis=-1, keepdims=True)
        sel = lane == idx
        cur = jnp.where(sel, -jnp.inf, cur)
        vals.append(m)
        idxs.append(idx)
        sels.append(sel)
    es = [jnp.exp(v - vals[0]) for v in vals]
    esum = es[0] + es[1] + es[2] + es[3]
    assign = jnp.where(sels[0] | sels[1] | sels[2] | sels[3], 1.0, 0.0)
    ii = lax.broadcasted_iota(jnp.int32, (TM, TM), 0)
    jj = lax.broadcasted_iota(jnp.int32, (TM, TM), 1)
    before = _dot(jnp.where(ii > jj, 1.0, 0.0), assign) + carry_ref[...]
    info = jnp.zeros(logits.shape, F32)
    for k in range(TOP_K):
        rank = jnp.sum(jnp.where(sels[k], before, 0.0), axis=-1, keepdims=True)
        info = jnp.where(lane == k, idxs[k].astype(F32), info)
        info = jnp.where(lane == TOP_K + k, rank, info)
        info = jnp.where(lane == 2 * TOP_K + k, es[k] / esum, info)
    ri_ref[...] = info
    carry_ref[...] = carry_ref[...] + jnp.sum(assign, axis=0, keepdims=True)
    cnt_ref[...] = carry_ref[...]


def _outproj(kind, mixer_out, xt, mods, w_out, gf, rw, rb, nct, out_scale=1.0):
    nb, t_all, d = xt.shape
    nt = t_all // TM
    tile = lambda w: pl.BlockSpec((None, TM, w), lambda b, t: (b, t, 0))
    full = lambda a: pl.BlockSpec(a.shape, lambda b, t: (0,) * a.ndim)
    if kind in (0, 1):
        of, ob, pm, zoff, hg = mixer_out
        zblk = zoff // d
        lead = [of, ob, pm, hg]
        lead_specs = [tile(d), tile(d), pl.BlockSpec((None, TM, d), lambda b, t: (b, t, zblk)), full(hg)]
    elif kind == 2:
        o, hg = mixer_out
        lead = [o, hg]
        lead_specs = [tile(d), full(hg)]
    else:
        lead = [mixer_out]
        lead_specs = [tile(d)]
    args = lead + [xt, mods, w_out, gf, rw, rb]
    in_specs = lead_specs + [tile(d), full(mods), full(w_out), full(gf), full(rw), full(rb)]
    return pl.pallas_call(
        functools.partial(_outproj_kernel, kind, nct, nb, d, out_scale),
        out_shape=(jax.ShapeDtypeStruct((nb, t_all, d), F32), jax.ShapeDtypeStruct((nb, t_all, d), F32),
                   jax.ShapeDtypeStruct((nb, t_all, LANES), F32), jax.ShapeDtypeStruct((1, LANES), F32)),
        grid=(nb, nt), in_specs=in_specs,
        out_specs=(tile(d), tile(d), tile(LANES), pl.BlockSpec((1, LANES), lambda b, t: (0, 0))),
        scratch_shapes=[pltpu.VMEM((1, LANES), F32)],
        compiler_params=_cp(("arbitrary", "arbitrary")), name=f"outproj{kind}",
    )(*args)


def _dispatch_kernel(zstart_ref, nused_ref, pos_ref, f_ref, xs_out, zeros, sem, zsem):
    @pl.when(pl.program_id(0) == 0)
    def _():
        zeros[...] = jnp.zeros_like(zeros)

        def zero_copy(row0):
            return pltpu.make_async_copy(zeros, xs_out.at[pl.ds(pl.multiple_of(row0, MOE_BM), MOE_BM)], zsem)

        n_blk = xs_out.shape[0] // MOE_BM
        for e in range(zstart_ref.shape[0]):
            pl.when(zstart_ref[e] >= 0)(lambda e=e: zero_copy(zstart_ref[e]).start())
        lax.fori_loop(nused_ref[0], n_blk, lambda b, c: (zero_copy(b * MOE_BM).start(), c)[1], 0)
        for e in range(zstart_ref.shape[0]):
            pl.when(zstart_ref[e] >= 0)(lambda e=e: zero_copy(zstart_ref[e]).wait())
        lax.fori_loop(nused_ref[0], n_blk, lambda b, c: (zero_copy(b * MOE_BM).wait(), c)[1], 0)

    def row_copy(r, k):
        return pltpu.make_async_copy(f_ref.at[pl.ds(r, 1)], xs_out.at[pl.ds(pos_ref[TOP_K * r + k], 1)], sem)

    def start(r, c):
        for k in range(TOP_K):
            row_copy(r, k).start(priority=k % 2)
        return c

    def wait(r, c):
        for k in range(TOP_K):
            row_copy(r, k).wait()
        return c

    lax.fori_loop(0, TM, start, 0)
    lax.fori_loop(0, TM, wait, 0)


def _dispatch(f2, pos_flat, zstart, n_used, n_rows):
    n, d = f2.shape
    grid_spec = pltpu.PrefetchScalarGridSpec(
        num_scalar_prefetch=2, grid=(n // TM,),
        in_specs=[pl.BlockSpec((TOP_K * TM,), lambda i, z, u: (i,), memory_space=pltpu.SMEM),
                  pl.BlockSpec((TM, d), lambda i, z, u: (i, 0))],
        out_specs=pl.BlockSpec(memory_space=pl.ANY),
        scratch_shapes=[pltpu.VMEM((MOE_BM, d), F32), pltpu.SemaphoreType.DMA, pltpu.SemaphoreType.DMA])
    return pl.pallas_call(
        _dispatch_kernel, out_shape=jax.ShapeDtypeStruct((n_rows, d), F32), grid_spec=grid_spec,
        compiler_params=_cp(("arbitrary",)), name="moe_dispatch",
    )(zstart, n_used, pos_flat, f2)


def _expert_kernel(be_ref, nused_ref, xs_ref, wgu_ref, bgu_ref, wd_ref, bd_ref, ys_ref, wgu_bf, wd_bf):
    i = pl.program_id(0)
    ff = wd_ref.shape[0]

    @pl.when(i >= nused_ref[0])
    def _():
        ys_ref[...] = jnp.zeros_like(ys_ref)

    @pl.when(i < nused_ref[0])
    def _():
        @pl.when(jnp.logical_or(i == 0, be_ref[i] != be_ref[jnp.maximum(i - 1, 0)]))
        def _():
            wgu_bf[...] = wgu_ref[...].astype(BF16)
            wd_bf[...] = wd_ref[...].astype(BF16)

        gu = _dot(xs_ref[...], wgu_bf[...]) + bgu_ref[...]
        x_glu = jnp.minimum(gu[:, :ff], SWIGLU_LIMIT)
        x_lin = jnp.clip(gu[:, ff:], -SWIGLU_LIMIT, SWIGLU_LIMIT)
        act = x_glu * _sigmoid(SWIGLU_ALPHA * x_glu) * (x_lin + 1.0)
        ys_ref[...] = _dot(act, wd_bf[...]) + bd_ref[...]


def _expert_ffn(xs, blk_exp, n_used, layer, w_gu, b_gu, w_down, b_down):
    n_rows, d = xs.shape
    _, ne, _, f2 = w_gu.shape
    ff = w_down.shape[2]
    grid_spec = pltpu.PrefetchScalarGridSpec(
        num_scalar_prefetch=2, grid=(n_rows // MOE_BM,),
        in_specs=[pl.BlockSpec((MOE_BM, d), lambda i, be, nu: (i, 0)),
                  pl.BlockSpec((None, None, d, f2), lambda i, be, nu: (layer, be[i], 0, 0)),
                  pl.BlockSpec((None, 1, f2), lambda i, be, nu: (be[i], 0, 0)),
                  pl.BlockSpec((None, None, ff, d), lambda i, be, nu: (layer, be[i], 0, 0)),
                  pl.BlockSpec((None, 1, d), lambda i, be, nu: (be[i], 0, 0))],
        out_specs=pl.BlockSpec((MOE_BM, d), lambda i, be, nu: (i, 0)),
        scratch_shapes=[pltpu.VMEM((d, f2), BF16), pltpu.VMEM((ff, d), BF16)])
    return pl.pallas_call(
        _expert_kernel, out_shape=jax.ShapeDtypeStruct((n_rows, d), F32), grid_spec=grid_spec,
        compiler_params=_cp(("arbitrary",)), name="moe_experts",
    )(blk_exp, n_used, xs, w_gu, b_gu.reshape(ne, 1, f2), w_down, b_down.reshape(ne, 1, d))


def _combine_kernel(nct, nb, nt, d, pos_ref, ri_ref, x1_ref, m_ref, ys_hbm, x2_ref, buf, sem):
    i = pl.program_id(0)
    b = i // nt
    t = i % nt
    row = jnp.where(t < nct, nb, b)

    def row_copy(r, k):
        return pltpu.make_async_copy(ys_hbm.at[pl.ds(pos_ref[TOP_K * r + k], 1)], buf.at[k, pl.ds(r, 1)], sem)

    def start(r, c):
        for k in range(TOP_K):
            row_copy(r, k).start(priority=k % 2)
        return c

    def wait(r, c):
        for k in range(TOP_K):
            row_copy(r, k).wait()
        return c

    lax.fori_loop(0, TM, start, 0)
    lax.fori_loop(0, TM, wait, 0)
    ri = ri_ref[...]
    y = None
    for k in range(TOP_K):
        term = buf[k] * ri[:, 2 * TOP_K + k:2 * TOP_K + k + 1]
        y = term if y is None else y + term
    x2_ref[...] = x1_ref[...] + _mod_row(m_ref, row, 5, d) * y


def _combine(ys, pos_flat, rinfo2, x1_2, mods, nct, nb):
    n, d = x1_2.shape
    nt = n // TM // nb
    return pl.pallas_call(
        functools.partial(_combine_kernel, nct, nb, nt, d),
        out_shape=jax.ShapeDtypeStruct((n, d), F32),
        grid=(n // TM,),
        in_specs=[pl.BlockSpec((TOP_K * TM,), lambda i: (i,), memory_space=pltpu.SMEM),
                  pl.BlockSpec((TM, LANES), lambda i: (i, 0)),
                  pl.BlockSpec((TM, d), lambda i: (i, 0)),
                  pl.BlockSpec(mods.shape, lambda i: (0, 0)),
                  pl.BlockSpec(memory_space=pl.ANY)],
        out_specs=pl.BlockSpec((TM, d), lambda i: (i, 0)),
        scratch_shapes=[pltpu.VMEM((TOP_K, TM, d), F32), pltpu.SemaphoreType.DMA],
        compiler_params=_cp(("arbitrary",)), name="moe_combine",
    )(pos_flat, rinfo2, x1_2, mods, ys)


def _moe(f, rinfo, counts, x1, mods, layer, w_gu, b_gu, w_down, b_down, nct):
    nb, t_all, d = f.shape
    n = nb * t_all
    ne = w_gu.shape[1]
    ri2 = rinfo.reshape(n, LANES)
    idx = ri2[:, :TOP_K].astype(jnp.int32)
    rank = ri2[:, TOP_K:2 * TOP_K].astype(jnp.int32)
    cnt = counts[0, :ne].astype(jnp.int32)
    padded = (cnt + MOE_BM - 1) // MOE_BM * MOE_BM
    pad_end = jnp.cumsum(padded)
    pad_start = pad_end - padded
    onehot = idx[:, :, None] == jnp.arange(ne, dtype=jnp.int32)[None, None, :]
    pos = (rank + jnp.sum(jnp.where(onehot, pad_start[None, None, :], 0), axis=-1)).reshape(-1)
    n_blk = (n * TOP_K + MOE_BM - 1) // MOE_BM + ne
    n_used = (pad_end[-1:] // MOE_BM).astype(jnp.int32)
    blk_row = jnp.minimum(jnp.arange(n_blk, dtype=jnp.int32), n_used - 1)
    blk_exp = jnp.sum(blk_row[:, None] * MOE_BM >= pad_end[None, :], axis=-1).astype(jnp.int32)
    zstart = jnp.where(padded > 0, pad_end - MOE_BM, -1).astype(jnp.int32)
    xs = _dispatch(f.reshape(n, d), pos, zstart, n_used, n_blk * MOE_BM)
    ys = _expert_ffn(xs, blk_exp, n_used, layer, w_gu, b_gu, w_down, b_down)
    x2 = _combine(ys, pos, ri2, x1.reshape(n, d), mods, nct, nb)
    return x2.reshape(nb, t_all, d)


def _final_kernel(x_ref, g_ref, o_ref):
    x = x_ref[...]
    o_ref[...] = x * lax.rsqrt(jnp.mean(x * x, axis=-1, keepdims=True) + NORM_EPS) * g_ref[...]


def _final_norm(xt, g, nct):
    nb, t_all, d = xt.shape
    seq = t_all - nct * TM
    return pl.pallas_call(
        _final_kernel, out_shape=jax.ShapeDtypeStruct((nb, seq, d), F32), grid=(nb, seq // TM),
        in_specs=[pl.BlockSpec((None, TM, d), lambda b, t: (b, t + nct, 0)), pl.BlockSpec(g.shape, lambda b, t: (0, 0))],
        out_specs=pl.BlockSpec((None, TM, d), lambda b, t: (b, t, 0)),
        compiler_params=_cp(("arbitrary", "arbitrary")), name="final_norm",
    )(xt, g)


def _rope_tables(seq, lc):
    rows = seq // GRID_W
    row = jnp.repeat(jnp.arange(rows), GRID_W).astype(F32)
    col = jnp.tile(jnp.arange(GRID_W), rows).astype(F32)
    inv = ROPE_BASE ** (-jnp.arange(16, dtype=F32) / 16)
    ang_r = row[:, None] * inv[None, :]
    ang_c = col[:, None] * inv[None, :]
    cos = jnp.concatenate([jnp.cos(ang_r)] * 2 + [jnp.cos(ang_c)] * 2, axis=1)
    sin = jnp.concatenate([-jnp.sin(ang_r), jnp.sin(ang_r), -jnp.sin(ang_c), jnp.sin(ang_c)], axis=1)
    cos = jnp.concatenate([jnp.ones((lc, 64), F32), cos], axis=0)
    sin = jnp.concatenate([jnp.zeros((lc, 64), F32), sin], axis=0)
    return jnp.tile(cos, (1, 2)), jnp.tile(sin, (1, 2))


def _pad_lanes(a, fill=0.0):
    return jnp.pad(a, [(0, 0)] * (a.ndim - 1) + [(0, LANES - a.shape[-1])], constant_values=fill)


def _gate_consts(first, second):
    return _pad_lanes(jnp.concatenate([first.reshape(1, -1), second.reshape(1, -1)], axis=1))


def kernel(x, c, ctx, c_ctx, ada_w, ada_b, norm_mix_g, norm_ffn_g, final_g, gdn_w_in, gdn_conv_w, gdn_a_log, gdn_dt_bias, gdn_norm_g, gdn_w_out, mlstm_w_in, mlstm_b_i, mlstm_b_f, mlstm_norm_g, mlstm_w_out, diff_w_in, diff_lambda, diff_norm_g, diff_w_out, swa_w_in, swa_sink, swa_w_out, router_w, router_b, moe_w_gu, moe_b_gu, moe_w_down, moe_b_down):
    nb, seq, d = x.shape
    lc = ctx.shape[1]
    depth = ada_w.shape[0]
    assert lc % TM == 0 and seq % TM == 0 and nb < 8
    nct = lc // TM
    xt = jnp.concatenate([ctx, x], axis=1)
    cond8 = jnp.zeros((8, d), F32).at[:nb].set(c).at[nb].set(c_ctx)
    mods_all = _ada_mods(cond8, ada_w, ada_b)
    cos, sin = _rope_tables(seq, lc)
    for i in range(depth):
        kind, j = i % 4, i // 4
        mods = mods_all[i]
        gm = norm_mix_g[i].reshape(1, d)
        gf = norm_ffn_g[i].reshape(1, d)
        rw = _pad_lanes(router_w[i])
        rb = _pad_lanes(router_b[i].reshape(1, -1), -jnp.inf)
        if kind == 0:
            w = gdn_w_in[j]
            main = w.shape[1] - 32
            p, gt, gc = _inproj(0, xt, gm, mods, w[:, :main].astype(BF16),
                                (_pad_lanes(w[:, main:]), _gate_consts(gdn_a_log[j], jnp.zeros_like(gdn_a_log[j])),
                                 _gate_consts(gdn_dt_bias[j], jnp.zeros_like(gdn_dt_bias[j]))), nct)
            q, k, v = _gdn_prep(p, jnp.pad(gdn_conv_w[j], ((0, 3), (0, 0))), nct)
            of, ob = _gdn_scan(q, k, v, gt, gc, nct)
            mixer_out = (of, ob, p, 3 * d, gdn_norm_g[j].reshape(1, -1))
            w_out, scale = gdn_w_out[j], 1.0
        elif kind == 1:
            w = mlstm_w_in[j]
            main = w.shape[1] - 32
            p, gt, gc = _inproj(1, xt, gm, mods, w[:, :main].astype(BF16),
                                (_pad_lanes(w[:, main:]), _gate_consts(mlstm_b_i[j], jnp.zeros_like(mlstm_b_i[j])),
                                 _pad_lanes(jnp.concatenate([jnp.zeros((1, 16), F32), mlstm_b_f[j].reshape(1, -1)], axis=1))),
                                nct)
            of, ob = _mlstm_scan(p, d // 2, d, gt, gc, nct)
            mixer_out = (of, ob, p, 2 * d, mlstm_norm_g[j].reshape(1, -1))
            w_out, scale = mlstm_w_out[j], 1.0
        elif kind == 2:
            lambda_init = 0.8 - 0.6 * math.exp(-0.3 * i)
            q, k, v = _inproj(2, xt, gm, mods, diff_w_in[j].astype(BF16), (cos, sin, (d, d, d)), nct)
            o = jnp.concatenate([_diff_attn(q[:, :lc], k[:, :lc], v[:, :lc], diff_lambda[j], lambda_init),
                                 _diff_attn(q[:, lc:], k, v, diff_lambda[j], lambda_init)], axis=1)
            mixer_out = (o, diff_norm_g[j].reshape(1, -1))
            w_out, scale = diff_w_out[j], 1.0 - lambda_init
        else:
            q, k, v = _inproj(3, xt, gm, mods, swa_w_in[j].astype(BF16), (cos, sin, (d, d // 4, d // 4)), nct)
            mixer_out = _swa_attn(q, k, v, _pad_lanes(swa_sink[j].reshape(1, -1)), nct, seq)
            w_out, scale = swa_w_out[j], 1.0
        x1, f, rinfo, counts = _outproj(kind, mixer_out, xt, mods, w_out.astype(BF16), gf, rw, rb, nct, scale)
        xt = _moe(f, rinfo, counts, x1, mods, i, moe_w_gu, moe_b_gu[i], moe_w_down, moe_b_down[i], nct)
    return _final_norm(xt, final_g.reshape(1, d), nct)
```

```python
import functools
import math

import jax
import jax.numpy as jnp
from jax import lax
from jax.experimental import pallas as pl
from jax.experimental.pallas import tpu as pltpu

F32 = jnp.float32
BF16 = jnp.bfloat16
HI = lax.Precision.HIGHEST

NORM_EPS = 1e-6
ROPE_BASE = 10000.0
GRID_W = 64
TM = 256
CHUNK = 64
CHUNK_SHIFT = 6
GDN_HEADS_PER_STEP = 4
DIFF_QK_AHEAD = 3
Q_SCALE = {2: 0.125 * math.log2(math.e), 3: 0.125}
LANES = 128
N_EXPERTS = 32
TOP_K = 4
SWIGLU_ALPHA = 1.702
SWIGLU_LIMIT = 7.0
MOE_BM = 256
SWA_WINDOW = 128
VMEM_LIMIT = 56 * 1024 * 1024


def _cp(sem, vmem=VMEM_LIMIT):
    return pltpu.CompilerParams(dimension_semantics=sem, vmem_limit_bytes=vmem)


def _dot(a, b):
    return jnp.dot(a.astype(BF16), b.astype(BF16), preferred_element_type=F32)


def _dot_nt(a, b):
    return lax.dot_general(a.astype(BF16), b.astype(BF16), (((1,), (1,)), ((), ())),
                           preferred_element_type=F32)


def _dot_tn(a, b):
    return lax.dot_general(a.astype(BF16), b.astype(BF16), (((0,), (0,)), ((), ())),
                           preferred_element_type=F32)


def _dot_hi(a, b):
    return jnp.dot(a, b, precision=HI, preferred_element_type=F32)


def _sigmoid(x):
    return 1.0 / (1.0 + jnp.exp(-x))


def _softplus(x):
    return jnp.maximum(x, 0.0) + jnp.log(1.0 + jnp.exp(-jnp.abs(x)))


def _mods_kernel(c_ref, w_ref, b_ref, o_ref):
    c = c_ref[...]
    o_ref[...] = _dot_hi(c * _sigmoid(c), w_ref[...]) + b_ref[...]


def _ada_mods(cond8, ada_w, ada_b):
    depth, d, d6 = ada_w.shape
    return pl.pallas_call(
        _mods_kernel,
        out_shape=jax.ShapeDtypeStruct((depth, 8, d6), F32),
        grid=(depth, d6 // d),
        in_specs=[pl.BlockSpec((8, d), lambda i, j: (0, 0)),
                  pl.BlockSpec((None, d, d), lambda i, j: (i, 0, j)),
                  pl.BlockSpec((None, 1, d), lambda i, j: (i, 0, j))],
        out_specs=pl.BlockSpec((None, 8, d), lambda i, j: (i, 0, j)),
        compiler_params=_cp(("arbitrary", "arbitrary")),
        name="ada_mods",
    )(cond8, ada_w, ada_b.reshape(depth, 1, d6))


def _mod_row(m_ref, row, k, d):
    return m_ref[pl.ds(row, 1), k * d:(k + 1) * d]


def _prenorm_mod(x, g, shift, scale):
    ms = jnp.mean(x * x, axis=-1, keepdims=True)
    y = x * lax.rsqrt(ms + NORM_EPS) * g
    return y * (1.0 + scale) + shift


def _rope_slab(x, cos, sin, first_half):
    part = jnp.where(first_half, pltpu.roll(x, LANES - 16, 1), pltpu.roll(x, 16, 1))
    return x * cos + part * sin


def _inproj_kernel(kind, nct, nb, d, *refs):
    b = pl.program_id(0)
    t = pl.program_id(1)
    row = jnp.where(t < nct, nb, b)
    x_ref, g_ref, m_ref, w_ref = refs[:4]
    rest = refs[4:]
    h = _prenorm_mod(x_ref[...], g_ref[...], _mod_row(m_ref, row, 0, d), _mod_row(m_ref, row, 1, d))
    p = _dot(h, w_ref[...])
    if kind in (0, 1):
        wg_ref, c0_ref, c1_ref, p_ref, gt_ref, gc_ref = rest
        p_ref[...] = p
        graw = _dot3(h, wg_ref[...])
        lane = lax.broadcasted_iota(jnp.int32, graw.shape, 1)
        if kind == 0:
            gate = jnp.where(lane < 16, -jnp.exp(c0_ref[...]) * _softplus(graw + c1_ref[...]),
                             _sigmoid(graw))
        else:
            gate = jnp.where(lane < 16, graw + c0_ref[...], -_softplus(-(graw + c1_ref[...])))
        ii = lax.broadcasted_iota(jnp.int32, (TM, TM), 0)
        jj = lax.broadcasted_iota(jnp.int32, (TM, TM), 1)
        same = jnp.right_shift(ii, CHUNK_SHIFT) == jnp.right_shift(jj, CHUNK_SHIFT)
        cum_f = _dot_mask(jnp.where(same, jnp.where(jj <= ii, 1.0, 0.0), 0.0), gate)
        cum_b = _dot_mask(jnp.where(same, jnp.where(jj >= ii, 1.0, 0.0), 0.0), gate)
        cum = jnp.where(lane % 16 < 8, cum_f, cum_b)
        decay_col = jnp.right_shift(lane, 4) == kind
        gate = jnp.where(decay_col, cum, gate)
        gt_ref[...] = gate
        gc_ref[...] = gate.T
    else:
        cos_ref, sin_ref, q_ref, k_ref, v_ref = rest
        cos = cos_ref[...]
        sin = sin_ref[...]
        lane = lax.broadcasted_iota(jnp.int32, cos.shape, 1)
        first_half = (lane % 32) < 16
        nq = q_ref.shape[-1] // LANES
        nk = k_ref.shape[-1] // LANES
        for s in range(nq):
            q_ref[:, s * LANES:(s + 1) * LANES] = (
                _rope_slab(p[:, s * LANES:(s + 1) * LANES], cos, sin, first_half) * Q_SCALE[kind]).astype(BF16)
        for s in range(nk):
            o = (nq + s) * LANES
            k_ref[:, s * LANES:(s + 1) * LANES] = _rope_slab(p[:, o:o + LANES], cos, sin, first_half).astype(BF16)
        o = (nq + nk) * LANES
        v_ref[...] = p[:, o:].astype(BF16)


def _inproj(kind, xt, g, mods, w_main, extra, nct):
    nb, t_all, d = xt.shape
    nt = t_all // TM
    pdim = w_main.shape[1]
    tile = lambda w: pl.BlockSpec((None, TM, w), lambda b, t: (b, t, 0))
    full = lambda a: pl.BlockSpec(a.shape, lambda b, t: (0,) * a.ndim)
    in_specs = [tile(d), full(g), full(mods), full(w_main)]
    if kind in (0, 1):
        wg, c0, c1 = extra
        in_specs += [full(wg), full(c0), full(c1)]
        out_shape = (jax.ShapeDtypeStruct((nb, t_all, pdim), F32),
                     jax.ShapeDtypeStruct((nb, t_all, LANES), F32),
                     jax.ShapeDtypeStruct((nb, LANES, t_all), F32))
        out_specs = (tile(pdim), tile(LANES), pl.BlockSpec((None, LANES, TM), lambda b, t: (b, 0, t)))
        args = (xt, g, mods, w_main, wg, c0, c1)
    else:
        cos, sin, widths = extra
        in_specs += [pl.BlockSpec((TM, LANES), lambda b, t: (t, 0))] * 2
        out_shape = tuple(jax.ShapeDtypeStruct((nb, t_all, w), BF16) for w in widths)
        out_specs = tuple(tile(w) for w in widths)
        args = (xt, g, mods, w_main, cos, sin)
    return pl.pallas_call(
        functools.partial(_inproj_kernel, kind, nct, nb, d),
        out_shape=out_shape, grid=(nb, nt), in_specs=in_specs, out_specs=out_specs,
        compiler_params=_cp(("arbitrary", "arbitrary")), name=f"inproj{kind}",
    )(*args)


def _gdn_prep_kernel(nct, nt, p_ref, prev_ref, next_ref, w_ref, q_ref, k_ref, v_ref, win_ref):
    t = pl.program_id(1)
    prev_ok = jnp.logical_and(t != 0, t != nct)
    next_ok = jnp.logical_and(t != nct - 1, t != nt - 1)
    win_ref[0:8, :] = jnp.where(prev_ok, prev_ref[...], 0.0)
    win_ref[8:8 + TM, :] = p_ref[...]
    win_ref[8 + TM:, :] = jnp.where(next_ok, next_ref[...], 0.0)
    width = q_ref.shape[-1]
    for part, o_ref in enumerate((q_ref, k_ref, v_ref)):
        for h in range(width // LANES):
            c0 = part * width + h * LANES
            acc = None
            for j in range(5):
                term = win_ref[6 + j:6 + j + TM, c0:c0 + LANES] * w_ref[j:j + 1, c0:c0 + LANES]
                acc = term if acc is None else acc + term
            y = acc * _sigmoid(acc)
            if part < 2:
                y = y * lax.rsqrt(jnp.sum(y * y, axis=-1, keepdims=True) + NORM_EPS)
            if part == 0:
                y = y * (LANES ** -0.5)
            o_ref[:, h * LANES:(h + 1) * LANES] = y


def _gdn_prep(p_main, conv_w8, nct):
    nb, t_all, _ = p_main.shape
    nt = t_all // TM
    width = conv_w8.shape[1] // 3
    cw = 3 * width
    t8 = t_all // 8
    r8 = TM // 8
    tile = pl.BlockSpec((None, TM, width), lambda b, t: (b, t, 0))
    return pl.pallas_call(
        functools.partial(_gdn_prep_kernel, nct, nt),
        out_shape=tuple(jax.ShapeDtypeStruct((nb, t_all, width), F32) for _ in range(3)),
        grid=(nb, nt),
        in_specs=[pl.BlockSpec((None, TM, cw), lambda b, t: (b, t, 0)),
                  pl.BlockSpec((None, 8, cw), lambda b, t: (b, jnp.maximum(t * r8 - 1, 0), 0)),
                  pl.BlockSpec((None, 8, cw), lambda b, t: (b, jnp.minimum((t + 1) * r8, t8 - 1), 0)),
                  pl.BlockSpec(conv_w8.shape, lambda b, t: (0, 0))],
        out_specs=(tile, tile, tile),
        scratch_shapes=[pltpu.VMEM((TM + 16, cw), F32)],
        compiler_params=_cp(("arbitrary", "arbitrary")), name="gdn_prep",
    )(p_main, p_main, p_main, conv_w8)


def _split_bf16(a):
    hi = a.astype(BF16)
    return hi, (a - hi.astype(F32)).astype(BF16)


def _dot_mask(mask01, x):
    m = mask01.astype(BF16)
    x1 = x.astype(BF16)
    r1 = x - x1.astype(F32)
    x2 = r1.astype(BF16)
    x3 = (r1 - x2.astype(F32)).astype(BF16)
    d = lambda y: jnp.dot(m, y, preferred_element_type=F32)
    return d(x1) + (d(x2) + d(x3))


def _dot3(a, b):
    ah, al = _split_bf16(a)
    bh, bl = _split_bf16(b)
    d = lambda x, y: jnp.dot(x, y, preferred_element_type=F32)
    return d(ah, bh) + (d(ah, bl) + d(al, bh))


def _unit_triangular_inverses(mats, same):
    a8 = [jnp.where(same(3), a, 0.0) for a in mats]
    ts = [jnp.where(same(0), 1.0, 0.0) - x for x in a8]
    p2 = [_dot3(x, x) for x in a8]
    ts = [t + _dot3(t, p) for t, p in zip(ts, p2)]
    p4 = [_dot3(p, p) for p in p2]
    ts = [t + _dot3(t, p) for t, p in zip(ts, p4)]
    for sh in (4, 5, 6):
        offs = [jnp.where(same(sh), jnp.where(same(sh - 1), 0.0, a), 0.0) for a in mats]
        mids = [_dot3(o, t) for o, t in zip(offs, ts)]
        ts = [t - _dot3(t, m) for t, m in zip(ts, mids)]
    return ts


def _gdn_tiles(probs):
    ii = lax.broadcasted_iota(jnp.int32, (TM, TM), 0)
    jj = lax.broadcasted_iota(jnp.int32, (TM, TM), 1)
    same = lambda sh: jnp.right_shift(ii, sh) == jnp.right_shift(jj, sh)
    incl = {rev: jnp.where(same(CHUNK_SHIFT), jnp.where((ii <= jj) if rev else (ii >= jj), 1.0, 0.0), 0.0) > 0.5
            for rev in {p[8] for p in probs}}
    gams = [jnp.where(incl[rev], jnp.exp(jnp.where(incl[rev], gc - gr, 0.0)), 0.0)
            for (_, _, _, _, gc, gr, _, _, rev) in probs]
    kbs = [p[1] * p[3] for p in probs]
    mats = [jnp.where(same(0), 0.0, _dot_nt(kb, p[1]) * gam) for kb, p, gam in zip(kbs, probs, gams)]
    tinvs = _unit_triangular_inverses(mats, same)
    egs = [jnp.exp(p[4]) for p in probs]
    uws = [_dot3(t, jnp.concatenate([p[2] * p[3], kb * eg], axis=1)) for t, p, kb, eg in zip(tinvs, probs, kbs, egs)]
    qks = [_dot_nt(p[0], p[1]) * gam for p, gam in zip(probs, gams)]
    qgs = [p[0] * eg for p, eg in zip(probs, egs)]
    states = [p[6] for p in probs]
    nchunk = TM // CHUNK
    for n in range(nchunk):
        cs = [nchunk - 1 - n if p[8] else n for p in probs]
        sls = [slice(c * CHUNK, (c + 1) * CHUNK) for c in cs]
        gtots = [p[4][(c * CHUNK if p[8] else (c + 1) * CHUNK - 1):][:1] for p, c in zip(probs, cs)]
        wss = [_dot(uw[sl, LANES:], s) for uw, sl, s in zip(uws, sls, states)]
        qss = [_dot(qg[sl], s) for qg, sl, s in zip(qgs, sls, states)]
        vns = [uw[sl, :LANES] - ws for uw, sl, ws in zip(uws, sls, wss)]
        for p, sl, qs, qk, vn in zip(probs, sls, qss, qks, vns):
            p[7](sl, qs + _dot(qk[sl, sl], vn))
        states = [s * jnp.exp(gt) + _dot_tn(p[1][sl] * jnp.exp(gt - p[4][sl]), vn)
                  for s, gt, p, sl, vn in zip(states, gtots, probs, sls, vns)]
    return states


def _gate_col(gt, idx):
    lane = lax.broadcasted_iota(jnp.int32, gt.shape, 1)
    return jnp.sum(jnp.where(lane == idx, gt, 0.0), axis=1, keepdims=True)


def _gate_row(gc, idx):
    sub = lax.broadcasted_iota(jnp.int32, gc.shape, 0)
    return jnp.sum(jnp.where(sub == idx, gc, 0.0), axis=0, keepdims=True)


def _gdn_scan_kernel(qf, kf, vf, gtf, gcf, qb, kb, vb, gtb, gcb, of_ref, ob_ref, s_ref):
    hp = pl.program_id(1)

    @pl.when(pl.program_id(2) == 0)
    def _():
        s_ref[...] = jnp.zeros_like(s_ref)

    probs = []
    for rev, (q_r, k_r, v_r, gt_r, gc_r, o_r) in enumerate(
            ((qf, kf, vf, gtf, gcf, of_ref), (qb, kb, vb, gtb, gcb, ob_ref))):
        gt = gt_r[...]
        gc = gc_r[...]
        for j in range(GDN_HEADS_PER_STEP):
            head = hp * GDN_HEADS_PER_STEP + j
            cols = slice(j * LANES, (j + 1) * LANES)

            def store(sl, val, o_r=o_r, cols=cols):
                o_r[sl, cols] = val

            probs.append((q_r[:, cols], k_r[:, cols], v_r[:, cols], _gate_col(gt, 16 + rev * 8 + head),
                          _gate_col(gt, rev * 8 + head), _gate_row(gc, rev * 8 + head), s_ref[rev, j], store,
                          bool(rev)))
    states = _gdn_tiles(probs)
    for rev in range(2):
        for j in range(GDN_HEADS_PER_STEP):
            s_ref[rev, j] = states[rev * GDN_HEADS_PER_STEP + j]


def _scan_tiles(nct, nt):
    fwd = lambda s: s
    bwd = lambda s: jnp.where(s < nct, nct - 1 - s, nt - 1 - (s - nct))
    return fwd, bwd


def _gdn_scan(q, k, v, gt, gc, nct):
    nb, t_all, width = q.shape
    hw = GDN_HEADS_PER_STEP * LANES
    nt = t_all // TM
    fwd, bwd = _scan_tiles(nct, nt)

    def specs(tile_of):
        head = pl.BlockSpec((None, TM, hw), lambda b, h, s: (b, tile_of(s), h))
        return [head, head, head,
                pl.BlockSpec((None, TM, LANES), lambda b, h, s: (b, tile_of(s), 0)),
                pl.BlockSpec((None, LANES, TM), lambda b, h, s: (b, 0, tile_of(s)))]

    out = jax.ShapeDtypeStruct((nb, t_all, width), F32)
    return pl.pallas_call(
        _gdn_scan_kernel, out_shape=(out, out), grid=(nb, width // hw, nt),
        in_specs=specs(fwd) + specs(bwd),
        out_specs=(pl.BlockSpec((None, TM, hw), lambda b, h, s: (b, fwd(s), h)),
                   pl.BlockSpec((None, TM, hw), lambda b, h, s: (b, bwd(s), h))),
        scratch_shapes=[pltpu.VMEM((2, GDN_HEADS_PER_STEP, LANES, LANES), F32)],
        compiler_params=_cp(("arbitrary", "arbitrary", "arbitrary")), name="gdn_scan",
    )(q, k, v, gt, gc, q, k, v, gt, gc)


def _mlstm_tiles(probs):
    ii = lax.broadcasted_iota(jnp.int32, (TM, TM), 0)
    jj = lax.broadcasted_iota(jnp.int32, (TM, TM), 1)
    same_chunk = jnp.right_shift(ii, CHUNK_SHIFT) == jnp.right_shift(jj, CHUNK_SHIFT)
    incl = {rev: jnp.where(same_chunk, jnp.where((ii <= jj) if rev else (ii >= jj), 1.0, 0.0), 0.0) > 0.5
            for rev in {p[10] for p in probs}}
    nchunk = TM // CHUNK
    order = lambda p: [nchunk - 1 - n if p[10] else n for n in range(nchunk)]
    rows = lambda c: slice(c * CHUNK, (c + 1) * CHUNK)
    dlogs = [jnp.where(incl[p[10]], p[5] - p[6] + p[4], -jnp.inf) for p in probs]
    m_intras = [jnp.max(d, axis=-1, keepdims=True) for d in dlogs]
    qks = [_dot_nt(p[0], p[1]) for p in probs]
    btots, m_locs, c_locs = [], [], []
    for p in probs:
        bt, ml, cl = {}, {}, {}
        for c in order(p):
            last = c * CHUNK if p[10] else (c + 1) * CHUNK - 1
            bt[c] = p[5][last:last + 1]
            a_loc = bt[c] - p[5][rows(c)] + p[3][rows(c)]
            ml[c] = jnp.max(a_loc, axis=0, keepdims=True)
            cl[c] = _dot_tn(p[1][rows(c)] * jnp.exp(a_loc - ml[c]), p[2][rows(c)])
        btots.append(bt)
        m_locs.append(ml)
        c_locs.append(cl)
    new_states, c_ins, m_ins = [], [], []
    for p, bt, ml, cl in zip(probs, btots, m_locs, c_locs):
        c_st, m_st = p[7], p[8]
        ci, mi = {}, {}
        for c in order(p):
            ci[c], mi[c] = c_st, m_st
            m_new = jnp.maximum(bt[c] + m_st, ml[c])
            c_st = c_st * jnp.exp(bt[c] + m_st - m_new) + cl[c] * jnp.exp(ml[c] - m_new)
            m_st = m_new
        new_states.append((c_st, m_st))
        c_ins.append(ci)
        m_ins.append(mi)
    inters = [jnp.concatenate([_dot(p[0][rows(c)], ci[c]) for c in range(nchunk)], axis=0)
              for p, ci in zip(probs, c_ins)]
    for p, dlog, m_intra, qk, inter, mi in zip(probs, dlogs, m_intras, qks, inters, m_ins):
        m_in = jnp.concatenate([jnp.broadcast_to(mi[c], (CHUNK, 1)) for c in range(nchunk)], axis=0)
        log_inter = p[5] + m_in
        m_t = jnp.maximum(log_inter, m_intra)
        num = jnp.exp(log_inter - m_t) * inter + _dot(jnp.exp(dlog - m_t) * qk, p[2])
        dv = p[2].shape[1] // 2
        p[9](num[:, :dv] / jnp.maximum(jnp.abs(num[:, dv:dv + 1]), jnp.exp(-m_t)))
    return new_states


def _mlstm_scan_kernel(qf, kf, vf, gtf, gcf, qb, kb, vb, gtb, gcb, of_ref, ob_ref, c_ref, m_ref):
    hp = pl.program_id(1)

    @pl.when(pl.program_id(2) == 0)
    def _():
        c_ref[...] = jnp.zeros_like(c_ref)
        m_ref[...] = jnp.zeros_like(m_ref)

    dqk = qf.shape[-1] // 2
    dv = vf.shape[-1] // 2
    lane = lax.broadcasted_iota(jnp.int32, (TM, dv), 1)
    ones_pad = jnp.where(lane == 0, 1.0, 0.0)
    probs = []
    for rev, (q_r, k_r, v_r, gt_r, gc_r, o_r) in enumerate(
            ((qf, kf, vf, gtf, gcf, of_ref), (qb, kb, vb, gtb, gcb, ob_ref))):
        gt = gt_r[...]
        gc = gc_r[...]
        for j in range(2):
            head = 2 * hp + j

            def store(val, o_r=o_r, j=j):
                o_r[:, j * dv:(j + 1) * dv] = val

            probs.append((q_r[:, j * dqk:(j + 1) * dqk], k_r[:, j * dqk:(j + 1) * dqk] * (dqk ** -0.5),
                          jnp.concatenate([v_r[:, j * dv:(j + 1) * dv], ones_pad], axis=1),
                          _gate_col(gt, rev * 8 + head), _gate_row(gc, rev * 8 + head),
                          _gate_col(gt, 16 + rev * 8 + head), _gate_row(gc, 16 + rev * 8 + head),
                          c_ref[rev, j], m_ref[rev, j][0:1, 0:1], store, bool(rev)))
    states = _mlstm_tiles(probs)
    for rev in range(2):
        for j in range(2):
            c_st, m_st = states[rev * 2 + j]
            c_ref[rev, j] = c_st
            m_ref[rev, j] = jnp.broadcast_to(m_st, (8, LANES))


def _mlstm_scan(p, qw, vw, gt, gc, nct):
    nb, t_all, _ = p.shape
    nhp = qw // LANES
    nt = t_all // TM
    dqk = LANES // 2
    dv = vw // nhp // 2
    fwd, bwd = _scan_tiles(nct, nt)

    def specs(tile_of):
        return [pl.BlockSpec((None, TM, LANES), lambda b, h, s: (b, tile_of(s), h)),
                pl.BlockSpec((None, TM, LANES), lambda b, h, s: (b, tile_of(s), nhp + h)),
                pl.BlockSpec((None, TM, 2 * dv), lambda b, h, s: (b, tile_of(s), 2 * qw // (2 * dv) + h)),
                pl.BlockSpec((None, TM, LANES), lambda b, h, s: (b, tile_of(s), 0)),
                pl.BlockSpec((None, LANES, TM), lambda b, h, s: (b, 0, tile_of(s)))]

    out = jax.ShapeDtypeStruct((nb, t_all, vw), F32)
    return pl.pallas_call(
        _mlstm_scan_kernel, out_shape=(out, out), grid=(nb, nhp, nt),
        in_specs=specs(fwd) + specs(bwd),
        out_specs=(pl.BlockSpec((None, TM, 2 * dv), lambda b, h, s: (b, fwd(s), h)),
                   pl.BlockSpec((None, TM, 2 * dv), lambda b, h, s: (b, bwd(s), h))),
        scratch_shapes=[pltpu.VMEM((2, 2, dqk, 2 * dv), F32), pltpu.VMEM((2, 2, 8, LANES), F32)],
        compiler_params=_cp(("arbitrary", "arbitrary", "arbitrary")), name="mlstm_scan",
    )(p, p, p, gt, gc, p, p, p, gt, gc)


def _diff_attn_kernel(nk, lambda_init, lam_ref, q_ref, k_ref, v_ref, o_ref, m_ref, acc_ref):
    j = pl.program_id(3)
    dh = LANES // 2

    @pl.when(j == 0)
    def _():
        m_ref[...] = jnp.full(m_ref.shape, -jnp.inf, F32)
        acc_ref[...] = jnp.zeros_like(acc_ref)

    tq = q_ref.shape[0]
    tk = k_ref.shape[0]
    ones_col = jnp.where(lax.broadcasted_iota(jnp.int32, (tk, LANES), 1) == 0, 1.0, 0.0).astype(BF16)
    v_aug = jnp.concatenate([v_ref[...], ones_col], axis=1)
    rs = 512 if tq % 512 == 0 else tq
    probs = [(hh, slice(r * rs, (r + 1) * rs)) for r in range(tq // rs) for hh in range(2)]

    def score(prob):
        hh, rows = prob
        return lax.dot_general(q_ref[rows, hh * dh:(hh + 1) * dh], k_ref[:, hh * dh:(hh + 1) * dh],
                               (((1,), (1,)), ((), ())), preferred_element_type=F32)

    scores = [score(p) for p in probs[:DIFF_QK_AHEAD]]
    blocks = [slice(c * LANES, (c + 1) * LANES) for c in range(tk // LANES)]
    for n, (hh, rows) in enumerate(probs):
        s = scores[n]
        if n + DIFF_QK_AHEAD < len(probs):
            scores.append(score(probs[n + DIFF_QK_AHEAD]))
        part = s[:, blocks[0]]
        for blk in blocks[1:]:
            part = jnp.maximum(part, s[:, blk])
        m_old = m_ref[hh, rows, :]
        m_new = jnp.maximum(m_old, jnp.broadcast_to(jnp.max(part, axis=-1, keepdims=True), (rs, LANES)))
        alpha = jnp.exp2(m_old - m_new)
        p = jnp.concatenate([jnp.exp2(s[:, blk] - m_new).astype(BF16) for blk in blocks], axis=1)
        acc_ref[hh, rows, :] = (jnp.concatenate([alpha, alpha], axis=1) * acc_ref[hh, rows, :]
                                + jnp.dot(p, v_aug, preferred_element_type=F32))
        m_ref[hh, rows, :] = m_new

    @pl.when(j == nk - 1)
    def _():
        lp = lam_ref[...]
        lam = (jnp.exp(jnp.sum(lp[0:1] * lp[1:2], axis=-1, keepdims=True))
               - jnp.exp(jnp.sum(lp[2:3] * lp[3:4], axis=-1, keepdims=True)) + lambda_init)
        a0 = acc_ref[0]
        a1 = acc_ref[1]
        o_ref[...] = a0[:, :LANES] / a0[:, LANES:LANES + 1] - lam * (a1[:, :LANES] / a1[:, LANES:LANES + 1])


def _pick_tile(n, candidates):
    return next(c for c in candidates if n % c == 0)


def _diff_attn(q, k, v, lam_p, lambda_init):
    nb, nq_rows, width = q.shape
    ns = k.shape[1]
    tq = _pick_tile(nq_rows, (2048, 1024, 512, 256))
    tk = _pick_tile(ns, (1280, 1024, 768, 512, 256))
    nk = ns // tk
    kv = pl.BlockSpec((None, tk, LANES), lambda b, h, i, j: (b, j, h))
    qo = pl.BlockSpec((None, tq, LANES), lambda b, h, i, j: (b, i, h))
    return pl.pallas_call(
        functools.partial(_diff_attn_kernel, nk, lambda_init),
        out_shape=jax.ShapeDtypeStruct((nb, nq_rows, width), F32),
        grid=(nb, width // LANES, nq_rows // tq, nk),
        in_specs=[pl.BlockSpec(lam_p.shape, lambda b, h, i, j: (0, 0)), qo, kv, kv],
        out_specs=qo,
        scratch_shapes=[pltpu.VMEM((2, tq, LANES), F32), pltpu.VMEM((2, tq, 2 * LANES), F32)],
        compiler_params=_cp(("arbitrary", "arbitrary", "arbitrary", "arbitrary")), name="diff_attn",
    )(lam_p, q, k, v)


def _swa_kernel(nct, nt, seq, sink_ref, q_ref, kc_ref, kp_ref, k0_ref, kn_ref, vc_ref, vp_ref, v0_ref, vn_ref,
                o_ref):
    i = pl.program_id(1)
    dh = LANES // 2
    nkv = kc_ref.shape[-1] // dh
    group = (q_ref.shape[-1] // dh) // nkv
    lc = kc_ref.shape[0]
    col = lax.broadcasted_iota(jnp.int32, (TM, lc + 3 * TM), 1)
    qpos = (i - nct) * TM + lax.broadcasted_iota(jnp.int32, (TM, lc + 3 * TM), 0)
    kpos = (i - nct - 1) * TM + col - lc
    ninf = -jnp.inf
    in_window = jnp.where(jnp.abs(qpos - kpos) <= SWA_WINDOW,
                          jnp.where(kpos >= 0, jnp.where(kpos < seq, 0.0, ninf), ninf), ninf)
    bias = jnp.where(col < lc, 0.0, jnp.where(i >= nct, in_window, ninf))
    kcat = jnp.concatenate([kc_ref[...], kp_ref[...], k0_ref[...], kn_ref[...]], axis=0)
    vcat = jnp.concatenate([vc_ref[...], vp_ref[...], v0_ref[...], vn_ref[...]], axis=0)
    sink = sink_ref[...]
    scores = [lax.dot_general(q_ref[:, hq * dh:(hq + 1) * dh], kcat[:, (hq // group) * dh:(hq // group + 1) * dh],
                              (((1,), (1,)), ((), ())), preferred_element_type=F32) for hq in range(nkv * group)]
    nkeys = lc + 3 * TM
    ones_col = jnp.where(lax.broadcasted_iota(jnp.int32, (nkeys, dh), 1) == 0, 1.0, 0.0).astype(BF16)
    v_aug = [jnp.concatenate([vcat[:, g * dh:(g + 1) * dh], ones_col], axis=1) for g in range(nkv)]
    blocks = [slice(c * LANES, (c + 1) * LANES) for c in range(nkeys // LANES)]
    biases = [bias[:, blk] for blk in blocks]
    for hq, s in enumerate(scores):
        sblk = [s[:, blk] + b for blk, b in zip(blocks, biases)]
        part = sblk[0]
        for x in sblk[1:]:
            part = jnp.maximum(part, x)
        sb = jnp.broadcast_to(sink[0:1, hq:hq + 1], (TM, LANES))
        m = jnp.maximum(jnp.broadcast_to(jnp.max(part, axis=-1, keepdims=True), (TM, LANES)), sb)
        e = jnp.concatenate([jnp.exp(x - m).astype(BF16) for x in sblk], axis=1)
        ov = jnp.dot(e, v_aug[hq // group], preferred_element_type=F32)
        denom = ov[:, dh:dh + 1] + jnp.exp(sb - m)[:, 0:1]
        o_ref[:, hq * dh:(hq + 1) * dh] = ov[:, :dh] / denom


def _swa_attn(q, k, v, sink_row, nct, seq):
    nb, t_all, qw = q.shape
    kw = k.shape[-1]
    nt = t_all // TM
    lc = nct * TM
    ctxs = pl.BlockSpec((None, lc, kw), lambda b, i: (b, 0, 0))
    prev = pl.BlockSpec((None, TM, kw), lambda b, i: (b, jnp.maximum(i - 1, 0), 0))
    cur = pl.BlockSpec((None, TM, kw), lambda b, i: (b, i, 0))
    nxt = pl.BlockSpec((None, TM, kw), lambda b, i: (b, jnp.minimum(i + 1, nt - 1), 0))
    qo = pl.BlockSpec((None, TM, qw), lambda b, i: (b, i, 0))
    return pl.pallas_call(
        functools.partial(_swa_kernel, nct, nt, seq),
        out_shape=jax.ShapeDtypeStruct((nb, t_all, qw), F32),
        grid=(nb, nt),
        in_specs=[pl.BlockSpec(sink_row.shape, lambda b, i: (0, 0)), qo, ctxs, prev, cur, nxt, ctxs, prev, cur, nxt],
        out_specs=qo,
        compiler_params=_cp(("arbitrary", "arbitrary")), name="swa_attn",
    )(sink_row, q, k, k, k, k, v, v, v, v)


def _headnorm(o, g_row):
    outs = []
    for h in range(o.shape[-1] // LANES):
        seg = o[:, h * LANES:(h + 1) * LANES]
        ms = jnp.mean(seg * seg, axis=-1, keepdims=True)
        outs.append(seg * lax.rsqrt(ms + NORM_EPS) * g_row)
    return jnp.concatenate(outs, axis=1)


def _outproj_kernel(kind, nct, nb, d, out_scale, *refs):
    b = pl.program_id(0)
    t = pl.program_id(1)
    row = jnp.where(t < nct, nb, b)
    if kind in (0, 1):
        of_ref, ob_ref, z_ref, hg_ref = refs[:4]
        rest = refs[4:]
        z = z_ref[...]
        o = _headnorm(of_ref[...] + ob_ref[...], hg_ref[...]) * (z * _sigmoid(z) if kind == 0 else _sigmoid(z))
    elif kind == 2:
        o_ref, hg_ref = refs[:2]
        rest = refs[2:]
        o = _headnorm(o_ref[...], hg_ref[...]) * out_scale
    else:
        o_ref = refs[0]
        rest = refs[1:]
        o = o_ref[...]
    x_ref, m_ref, w_ref, gf_ref, rw_ref, rb_ref, x1_ref, f_ref, ri_ref, cnt_ref, carry_ref = rest

    @pl.when(jnp.logical_and(b == 0, t == 0))
    def _():
        carry_ref[...] = jnp.zeros_like(carry_ref)

    x1 = x_ref[...] + _mod_row(m_ref, row, 2, d) * _dot(o, w_ref[...])
    x1_ref[...] = x1
    f = _prenorm_mod(x1, gf_ref[...], _mod_row(m_ref, row, 3, d), _mod_row(m_ref, row, 4, d))
    f_ref[...] = f
    logits = _dot3(f, rw_ref[...]) + rb_ref[...]
    lane = lax.broadcasted_iota(jnp.int32, logits.shape, 1)
    vals, idxs, sels = [], [], []
    cur = logits
    for _ in range(TOP_K):
        m = jnp.max(cur, axis=-1, keepdims=True)
        idx = jnp.min(jnp.where(cur == m, lane, LANES), axis=-1, keepdims=True)
        sel = lane == idx
        cur = jnp.where(sel, -jnp.inf, cur)
        vals.append(m)
        idxs.append(idx)
        sels.append(sel)
    es = [jnp.exp(v - vals[0]) for v in vals]
    esum = es[0] + es[1] + es[2] + es[3]
    assign = jnp.where(sels[0] | sels[1] | sels[2] | sels[3], 1.0, 0.0)
    ii = lax.broadcasted_iota(jnp.int32, (TM, TM), 0)
    jj = lax.broadcasted_iota(jnp.int32, (TM, TM), 1)
    before = _dot(jnp.where(ii > jj, 1.0, 0.0), assign) + carry_ref[...]
    info = jnp.zeros(logits.shape, F32)
    for k in range(TOP_K):
        rank = jnp.sum(jnp.where(sels[k], before, 0.0), axis=-1, keepdims=True)
        info = jnp.where(lane == k, idxs[k].astype(F32), info)
        info = jnp.where(lane == TOP_K + k, rank, info)
        info = jnp.where(lane == 2 * TOP_K + k, es[k] / esum, info)
    ri_ref[...] = info
    carry_ref[...] = carry_ref[...] + jnp.sum(assign, axis=0, keepdims=True)
    cnt_ref[...] = carry_ref[...]


def _outproj(kind, mixer_out, xt, mods, w_out, gf, rw, rb, nct, out_scale=1.0):
    nb, t_all, d = xt.shape
    nt = t_all // TM
    tile = lambda w: pl.BlockSpec((None, TM, w), lambda b, t: (b, t, 0))
    full = lambda a: pl.BlockSpec(a.shape, lambda b, t: (0,) * a.ndim)
    if kind in (0, 1):
        of, ob, pm, zoff, hg = mixer_out
        zblk = zoff // d
        lead = [of, ob, pm, hg]
        lead_specs = [tile(d), tile(d), pl.BlockSpec((None, TM, d), lambda b, t: (b, t, zblk)), full(hg)]
    elif kind == 2:
        o, hg = mixer_out
        lead = [o, hg]
        lead_specs = [tile(d), full(hg)]
    else:
        lead = [mixer_out]
        lead_specs = [tile(d)]
    args = lead + [xt, mods, w_out, gf, rw, rb]
    in_specs = lead_specs + [tile(d), full(mods), full(w_out), full(gf), full(rw), full(rb)]
    return pl.pallas_call(
        functools.partial(_outproj_kernel, kind, nct, nb, d, out_scale),
        out_shape=(jax.ShapeDtypeStruct((nb, t_all, d), F32), jax.ShapeDtypeStruct((nb, t_all, d), F32),
                   jax.ShapeDtypeStruct((nb, t_all, LANES), F32), jax.ShapeDtypeStruct((1, LANES), F32)),
        grid=(nb, nt), in_specs=in_specs,
        out_specs=(tile(d), tile(d), tile(LANES), pl.BlockSpec((1, LANES), lambda b, t: (0, 0))),
        scratch_shapes=[pltpu.VMEM((1, LANES), F32)],
        compiler_params=_cp(("arbitrary", "arbitrary")), name=f"outproj{kind}",
    )(*args)


def _dispatch_kernel(zstart_ref, nused_ref, pos_ref, f_ref, xs_out, zeros, sem, zsem):
    @pl.when(pl.program_id(0) == 0)
    def _():
        zeros[...] = jnp.zeros_like(zeros)

        def zero_copy(row0):
            return pltpu.make_async_copy(zeros, xs_out.at[pl.ds(pl.multiple_of(row0, MOE_BM), MOE_BM)], zsem)

        n_blk = xs_out.shape[0] // MOE_BM
        for e in range(zstart_ref.shape[0]):
            pl.when(zstart_ref[e] >= 0)(lambda e=e: zero_copy(zstart_ref[e]).start())
        lax.fori_loop(nused_ref[0], n_blk, lambda b, c: (zero_copy(b * MOE_BM).start(), c)[1], 0)
        for e in range(zstart_ref.shape[0]):
            pl.when(zstart_ref[e] >= 0)(lambda e=e: zero_copy(zstart_ref[e]).wait())
        lax.fori_loop(nused_ref[0], n_blk, lambda b, c: (zero_copy(b * MOE_BM).wait(), c)[1], 0)

    def row_copy(r, k):
        return pltpu.make_async_copy(f_ref.at[pl.ds(r, 1)], xs_out.at[pl.ds(pos_ref[TOP_K * r + k], 1)], sem)

    def start(r, c):
        for k in range(TOP_K):
            row_copy(r, k).start(priority=k % 2)
        return c

    def wait(r, c):
        for k in range(TOP_K):
            row_copy(r, k).wait()
        return c

    lax.fori_loop(0, TM, start, 0)
    lax.fori_loop(0, TM, wait, 0)


def _dispatch(f2, pos_flat, zstart, n_used, n_rows):
    n, d = f2.shape
    grid_spec = pltpu.PrefetchScalarGridSpec(
        num_scalar_prefetch=2, grid=(n // TM,),
        in_specs=[pl.BlockSpec((TOP_K * TM,), lambda i, z, u: (i,), memory_space=pltpu.SMEM),
                  pl.BlockSpec((TM, d), lambda i, z, u: (i, 0))],
        out_specs=pl.BlockSpec(memory_space=pl.ANY),
        scratch_shapes=[pltpu.VMEM((MOE_BM, d), F32), pltpu.SemaphoreType.DMA, pltpu.SemaphoreType.DMA])
    return pl.pallas_call(
        _dispatch_kernel, out_shape=jax.ShapeDtypeStruct((n_rows, d), F32), grid_spec=grid_spec,
        compiler_params=_cp(("arbitrary",)), name="moe_dispatch",
    )(zstart, n_used, pos_flat, f2)


def _expert_kernel(be_ref, nused_ref, xs_ref, wgu_ref, bgu_ref, wd_ref, bd_ref, ys_ref, wgu_bf, wd_bf):
    i = pl.program_id(0)
    ff = wd_ref.shape[0]

    @pl.when(i >= nused_ref[0])
    def _():
        ys_ref[...] = jnp.zeros_like(ys_ref)

    @pl.when(i < nused_ref[0])
    def _():
        @pl.when(jnp.logical_or(i == 0, be_ref[i] != be_ref[jnp.maximum(i - 1, 0)]))
        def _():
            wgu_bf[...] = wgu_ref[...].astype(BF16)
            wd_bf[...] = wd_ref[...].astype(BF16)

        gu = _dot(xs_ref[...], wgu_bf[...]) + bgu_ref[...]
        x_glu = jnp.minimum(gu[:, :ff], SWIGLU_LIMIT)
        x_lin = jnp.clip(gu[:, ff:], -SWIGLU_LIMIT, SWIGLU_LIMIT)
        act = x_glu * _sigmoid(SWIGLU_ALPHA * x_glu) * (x_lin + 1.0)
        ys_ref[...] = _dot(act, wd_bf[...]) + bd_ref[...]


def _expert_ffn(xs, blk_exp, n_used, layer, w_gu, b_gu, w_down, b_down):
    n_rows, d = xs.shape
    _, ne, _, f2 = w_gu.shape
    ff = w_down.shape[2]
    grid_spec = pltpu.PrefetchScalarGridSpec(
        num_scalar_prefetch=2, grid=(n_rows // MOE_BM,),
        in_specs=[pl.BlockSpec((MOE_BM, d), lambda i, be, nu: (i, 0)),
                  pl.BlockSpec((None, None, d, f2), lambda i, be, nu: (layer, be[i], 0, 0)),
                  pl.BlockSpec((None, 1, f2), lambda i, be, nu: (be[i], 0, 0)),
                  pl.BlockSpec((None, None, ff, d), lambda i, be, nu: (layer, be[i], 0, 0)),
                  pl.BlockSpec((None, 1, d), lambda i, be, nu: (be[i], 0, 0))],
        out_specs=pl.BlockSpec((MOE_BM, d), lambda i, be, nu: (i, 0)),
        scratch_shapes=[pltpu.VMEM((d, f2), BF16), pltpu.VMEM((ff, d), BF16)])
    return pl.pallas_call(
        _expert_kernel, out_shape=jax.ShapeDtypeStruct((n_rows, d), F32), grid_spec=grid_spec,
        compiler_params=_cp(("arbitrary",)), name="moe_experts",
    )(blk_exp, n_used, xs, w_gu, b_gu.reshape(ne, 1, f2), w_down, b_down.reshape(ne, 1, d))


def _combine_kernel(nct, nb, nt, d, pos_ref, ri_ref, x1_ref, m_ref, ys_hbm, x2_ref, buf, sem):
    i = pl.program_id(0)
    b = i // nt
    t = i % nt
    row = jnp.where(t < nct, nb, b)

    def row_copy(r, k):
        return pltpu.make_async_copy(ys_hbm.at[pl.ds(pos_ref[TOP_K * r + k], 1)], buf.at[k, pl.ds(r, 1)], sem)

    def start(r, c):
        for k in range(TOP_K):
            row_copy(r, k).start(priority=k % 2)
        return c

    def wait(r, c):
        for k in range(TOP_K):
            row_copy(r, k).wait()
        return c

    lax.fori_loop(0, TM, start, 0)
    lax.fori_loop(0, TM, wait, 0)
    ri = ri_ref[...]
    y = None
    for k in range(TOP_K):
        term = buf[k] * ri[:, 2 * TOP_K + k:2 * TOP_K + k + 1]
        y = term if y is None else y + term
    x2_ref[...] = x1_ref[...] + _mod_row(m_ref, row, 5, d) * y


def _combine(ys, pos_flat, rinfo2, x1_2, mods, nct, nb):
    n, d = x1_2.shape
    nt = n // TM // nb
    return pl.pallas_call(
        functools.partial(_combine_kernel, nct, nb, nt, d),
        out_shape=jax.ShapeDtypeStruct((n, d), F32),
        grid=(n // TM,),
        in_specs=[pl.BlockSpec((TOP_K * TM,), lambda i: (i,), memory_space=pltpu.SMEM),
                  pl.BlockSpec((TM, LANES), lambda i: (i, 0)),
                  pl.BlockSpec((TM, d), lambda i: (i, 0)),
                  pl.BlockSpec(mods.shape, lambda i: (0, 0)),
                  pl.BlockSpec(memory_space=pl.ANY)],
        out_specs=pl.BlockSpec((TM, d), lambda i: (i, 0)),
        scratch_shapes=[pltpu.VMEM((TOP_K, TM, d), F32), pltpu.SemaphoreType.DMA],
        compiler_params=_cp(("arbitrary",)), name="moe_combine",
    )(pos_flat, rinfo2, x1_2, mods, ys)


def _moe(f, rinfo, counts, x1, mods, layer, w_gu, b_gu, w_down, b_down, nct):
    nb, t_all, d = f.shape
    n = nb * t_all
    ne = w_gu.shape[1]
    ri2 = rinfo.reshape(n, LANES)
    idx = ri2[:, :TOP_K].astype(jnp.int32)
    rank = ri2[:, TOP_K:2 * TOP_K].astype(jnp.int32)
    cnt = counts[0, :ne].astype(jnp.int32)
    padded = (cnt + MOE_BM - 1) // MOE_BM * MOE_BM
    pad_end = jnp.cumsum(padded)
    pad_start = pad_end - padded
    onehot = idx[:, :, None] == jnp.arange(ne, dtype=jnp.int32)[None, None, :]
    pos = (rank + jnp.sum(jnp.where(onehot, pad_start[None, None, :], 0), axis=-1)).reshape(-1)
    n_blk = (n * TOP_K + MOE_BM - 1) // MOE_BM + ne
    n_used = (pad_end[-1:] // MOE_BM).astype(jnp.int32)
    blk_row = jnp.minimum(jnp.arange(n_blk, dtype=jnp.int32), n_used - 1)
    blk_exp = jnp.sum(blk_row[:, None] * MOE_BM >= pad_end[None, :], axis=-1).astype(jnp.int32)
    zstart = jnp.where(padded > 0, pad_end - MOE_BM, -1).astype(jnp.int32)
    xs = _dispatch(f.reshape(n, d), pos, zstart, n_used, n_blk * MOE_BM)
    ys = _expert_ffn(xs, blk_exp, n_used, layer, w_gu, b_gu, w_down, b_down)
    x2 = _combine(ys, pos, ri2, x1.reshape(n, d), mods, nct, nb)
    return x2.reshape(nb, t_all, d)


def _final_kernel(x_ref, g_ref, o_ref):
    x = x_ref[...]
    o_ref[...] = x * lax.rsqrt(jnp.mean(x * x, axis=-1, keepdims=True) + NORM_EPS) * g_ref[...]


def _final_norm(xt, g, nct):
    nb, t_all, d = xt.shape
    seq = t_all - nct * TM
    return pl.pallas_call(
        _final_kernel, out_shape=jax.ShapeDtypeStruct((nb, seq, d), F32), grid=(nb, seq // TM),
        in_specs=[pl.BlockSpec((None, TM, d), lambda b, t: (b, t + nct, 0)), pl.BlockSpec(g.shape, lambda b, t: (0, 0))],
        out_specs=pl.BlockSpec((None, TM, d), lambda b, t: (b, t, 0)),
        compiler_params=_cp(("arbitrary", "arbitrary")), name="final_norm",
    )(xt, g)


def _rope_tables(seq, lc):
    rows = seq // GRID_W
    row = jnp.repeat(jnp.arange(rows), GRID_W).astype(F32)
    col = jnp.tile(jnp.arange(GRID_W), rows).astype(F32)
    inv = ROPE_BASE ** (-jnp.arange(16, dtype=F32) / 16)
    ang_r = row[:, None] * inv[None, :]
    ang_c = col[:, None] * inv[None, :]
    cos = jnp.concatenate([jnp.cos(ang_r)] * 2 + [jnp.cos(ang_c)] * 2, axis=1)
    sin = jnp.concatenate([-jnp.sin(ang_r), jnp.sin(ang_r), -jnp.sin(ang_c), jnp.sin(ang_c)], axis=1)
    cos = jnp.concatenate([jnp.ones((lc, 64), F32), cos], axis=0)
    sin = jnp.concatenate([jnp.zeros((lc, 64), F32), sin], axis=0)
    return jnp.tile(cos, (1, 2)), jnp.tile(sin, (1, 2))


def _pad_lanes(a, fill=0.0):
    return jnp.pad(a, [(0, 0)] * (a.ndim - 1) + [(0, LANES - a.shape[-1])], constant_values=fill)


def _gate_consts(first, second):
    return _pad_lanes(jnp.concatenate([first.reshape(1, -1), second.reshape(1, -1)], axis=1))


def kernel(x, c, ctx, c_ctx, ada_w, ada_b, norm_mix_g, norm_ffn_g, final_g, gdn_w_in, gdn_conv_w, gdn_a_log, gdn_dt_bias, gdn_norm_g, gdn_w_out, mlstm_w_in, mlstm_b_i, mlstm_b_f, mlstm_norm_g, mlstm_w_out, diff_w_in, diff_lambda, diff_norm_g, diff_w_out, swa_w_in, swa_sink, swa_w_out, router_w, router_b, moe_w_gu, moe_b_gu, moe_w_down, moe_b_down):
    nb, seq, d = x.shape
    lc = ctx.shape[1]
    depth = ada_w.shape[0]
    assert lc % TM == 0 and seq % TM == 0 and nb < 8
    nct = lc // TM
    xt = jnp.concatenate([ctx, x], axis=1)
    cond8 = jnp.zeros((8, d), F32).at[:nb].set(c).at[nb].set(c_ctx)
    mods_all = _ada_mods(cond8, ada_w, ada_b)
    cos, sin = _rope_tables(seq, lc)
    for i in range(depth):
        kind, j = i % 4, i // 4
        mods = mods_all[i]
        gm = norm_mix_g[i].reshape(1, d)
        gf = norm_ffn_g[i].reshape(1, d)
        rw = _pad_lanes(router_w[i])
        rb = _pad_lanes(router_b[i].reshape(1, -1), -jnp.inf)
        if kind == 0:
            w = gdn_w_in[j]
            main = w.shape[1] - 32
            p, gt, gc = _inproj(0, xt, gm, mods, w[:, :main].astype(BF16),
                                (_pad_lanes(w[:, main:]), _gate_consts(gdn_a_log[j], jnp.zeros_like(gdn_a_log[j])),
                                 _gate_consts(gdn_dt_bias[j], jnp.zeros_like(gdn_dt_bias[j]))), nct)
            q, k, v = _gdn_prep(p, jnp.pad(gdn_conv_w[j], ((0, 3), (0, 0))), nct)
            of, ob = _gdn_scan(q, k, v, gt, gc, nct)
            mixer_out = (of, ob, p, 3 * d, gdn_norm_g[j].reshape(1, -1))
            w_out, scale = gdn_w_out[j], 1.0
        elif kind == 1:
            w = mlstm_w_in[j]
            main = w.shape[1] - 32
            p, gt, gc = _inproj(1, xt, gm, mods, w[:, :main].astype(BF16),
                                (_pad_lanes(w[:, main:]), _gate_consts(mlstm_b_i[j], jnp.zeros_like(mlstm_b_i[j])),
                                 _pad_lanes(jnp.concatenate([jnp.zeros((1, 16), F32), mlstm_b_f[j].reshape(1, -1)], axis=1))),
                                nct)
            of, ob = _mlstm_scan(p, d // 2, d, gt, gc, nct)
            mixer_out = (of, ob, p, 2 * d, mlstm_norm_g[j].reshape(1, -1))
            w_out, scale = mlstm_w_out[j], 1.0
        elif kind == 2:
            lambda_init = 0.8 - 0.6 * math.exp(-0.3 * i)
            q, k, v = _inproj(2, xt, gm, mods, diff_w_in[j].astype(BF16), (cos, sin, (d, d, d)), nct)
            o = jnp.concatenate([_diff_attn(q[:, :lc], k[:, :lc], v[:, :lc], diff_lambda[j], lambda_init),
                                 _diff_attn(q[:, lc:], k, v, diff_lambda[j], lambda_init)], axis=1)
            mixer_out = (o, diff_norm_g[j].reshape(1, -1))
            w_out, scale = diff_w_out[j], 1.0 - lambda_init
        else:
            q, k, v = _inproj(3, xt, gm, mods, swa_w_in[j].astype(BF16), (cos, sin, (d, d // 4, d // 4)), nct)
            mixer_out = _swa_attn(q, k, v, _pad_lanes(swa_sink[j].reshape(1, -1)), nct, seq)
            w_out, scale = swa_w_out[j], 1.0
        x1, f, rinfo, counts = _outproj(kind, mixer_out, xt, mods, w_out.astype(BF16), gf, rw, rb, nct, scale)
        xt = _moe(f, rinfo, counts, x1, mods, i, moe_w_gu, moe_b_gu[i], moe_w_down, moe_b_down[i], nct)
    return _final_norm(xt, final_g.reshape(1, d), nct)
```

```python
import functools
import math

import jax
import jax.numpy as jnp
from jax import lax
from jax.experimental import pallas as pl
from jax.experimental.pallas import tpu as pltpu

F32 = jnp.float32
BF16 = jnp.bfloat16
HI = lax.Precision.HIGHEST

NORM_EPS = 1e-6
ROPE_BASE = 10000.0
GRID_W = 64
TM = 256
CHUNK = 64
CHUNK_SHIFT = 6
GDN_HEADS_PER_STEP = 4
DIFF_ROWS = 512
DIFF_QK_AHEAD = 3
Q_SCALE = {2: 0.125 * math.log2(math.e), 3: 0.125}
LANES = 128
N_EXPERTS = 32
TOP_K = 4
SWIGLU_ALPHA = 1.702
SWIGLU_LIMIT = 7.0
MOE_BM = 256
SWA_WINDOW = 128
VMEM_LIMIT = 56 * 1024 * 1024


def _cp(sem, vmem=VMEM_LIMIT):
    return pltpu.CompilerParams(dimension_semantics=sem, vmem_limit_bytes=vmem)


def _dot(a, b):
    return jnp.dot(a.astype(BF16), b.astype(BF16), preferred_element_type=F32)


def _dot_nt(a, b):
    return lax.dot_general(a.astype(BF16), b.astype(BF16), (((1,), (1,)), ((), ())),
                           preferred_element_type=F32)


def _dot_tn(a, b):
    return lax.dot_general(a.astype(BF16), b.astype(BF16), (((0,), (0,)), ((), ())),
                           preferred_element_type=F32)


def _dot_hi(a, b):
    return jnp.dot(a, b, precision=HI, preferred_element_type=F32)


def _sigmoid(x):
    return 1.0 / (1.0 + jnp.exp(-x))


def _softplus(x):
    return jnp.maximum(x, 0.0) + jnp.log(1.0 + jnp.exp(-jnp.abs(x)))


def _mods_kernel(c_ref, w_ref, b_ref, o_ref):
    c = c_ref[...]
    o_ref[...] = _dot_hi(c * _sigmoid(c), w_ref[...]) + b_ref[...]


def _ada_mods(cond8, ada_w, ada_b):
    depth, d, d6 = ada_w.shape
    return pl.pallas_call(
        _mods_kernel,
        out_shape=jax.ShapeDtypeStruct((depth, 8, d6), F32),
        grid=(depth, d6 // d),
        in_specs=[pl.BlockSpec((8, d), lambda i, j: (0, 0)),
                  pl.BlockSpec((None, d, d), lambda i, j: (i, 0, j)),
                  pl.BlockSpec((None, 1, d), lambda i, j: (i, 0, j))],
        out_specs=pl.BlockSpec((None, 8, d), lambda i, j: (i, 0, j)),
        compiler_params=_cp(("arbitrary", "arbitrary")),
        name="ada_mods",
    )(cond8, ada_w, ada_b.reshape(depth, 1, d6))


def _mod_row(m_ref, row, k, d):
    return m_ref[pl.ds(row, 1), k * d:(k + 1) * d]


def _prenorm_mod(x, g, shift, scale):
    ms = jnp.mean(x * x, axis=-1, keepdims=True)
    y = x * lax.rsqrt(ms + NORM_EPS) * g
    return y * (1.0 + scale) + shift


def _rope_slab(x, cos, sin, first_half):
    part = jnp.where(first_half, pltpu.roll(x, LANES - 16, 1), pltpu.roll(x, 16, 1))
    return x * cos + part * sin


def _gather_rows(ys_hbm, pos_ref, buf, sem, slot, start):
    def row_copy(r, k):
        return pltpu.make_async_copy(ys_hbm.at[pl.ds(pos_ref[TOP_K * r + k], 1)], buf.at[slot, k, pl.ds(r, 1)],
                                     sem.at[slot])

    def body(r, c):
        for k in range(TOP_K):
            if start:
                row_copy(r, k).start(priority=k % 2)
            else:
                row_copy(r, k).wait()
        return c

    lax.fori_loop(0, TM, body, 0)


def _inproj_kernel(kind, nct, nb, d, fused, *refs):
    b = pl.program_id(0)
    t = pl.program_id(1)
    row = jnp.where(t < nct, nb, b)
    if fused:
        pos_ref, posn_ref, ri_ref, x1_ref, mprev_ref, ys_hbm = refs[:6]
        x2_ref, buf, sem = refs[-3:]
        refs = refs[6:-3]
        nt = pl.num_programs(1)
        i = b * nt + t
        slot = i % 2

        @pl.when(i == 0)
        def _():
            _gather_rows(ys_hbm, pos_ref, buf, sem, 0, True)

        @pl.when(i + 1 < nb * nt)
        def _():
            _gather_rows(ys_hbm, posn_ref, buf, sem, 1 - slot, True)

        _gather_rows(ys_hbm, pos_ref, buf, sem, slot, False)
        ri = ri_ref[...]
        y = None
        for k in range(TOP_K):
            term = buf[slot, k] * ri[:, 2 * TOP_K + k:2 * TOP_K + k + 1]
            y = term if y is None else y + term
        x = x1_ref[...] + _mod_row(mprev_ref, row, 5, d) * y
        x2_ref[...] = x
        g_ref, m_ref, w_ref = refs[:3]
        rest = refs[3:]
    else:
        x_ref, g_ref, m_ref, w_ref = refs[:4]
        rest = refs[4:]
        x = x_ref[...]
    h = _prenorm_mod(x, g_ref[...], _mod_row(m_ref, row, 0, d), _mod_row(m_ref, row, 1, d))
    p = _dot(h, w_ref[...])
    if kind in (0, 1):
        wg_ref, c0_ref, c1_ref, p_ref, gt_ref, gc_ref = rest
        p_ref[...] = p
        graw = _dot3(h, wg_ref[...])
        lane = lax.broadcasted_iota(jnp.int32, graw.shape, 1)
        if kind == 0:
            gate = jnp.where(lane < 16, -jnp.exp(c0_ref[...]) * _softplus(graw + c1_ref[...]),
                             _sigmoid(graw))
        else:
            gate = jnp.where(lane < 16, graw + c0_ref[...], -_softplus(-(graw + c1_ref[...])))
        ii = lax.broadcasted_iota(jnp.int32, (TM, TM), 0)
        jj = lax.broadcasted_iota(jnp.int32, (TM, TM), 1)
        same = jnp.right_shift(ii, CHUNK_SHIFT) == jnp.right_shift(jj, CHUNK_SHIFT)
        cum_f = _dot_mask(jnp.where(same, jnp.where(jj <= ii, 1.0, 0.0), 0.0), gate)
        cum_b = _dot_mask(jnp.where(same, jnp.where(jj >= ii, 1.0, 0.0), 0.0), gate)
        cum = jnp.where(lane % 16 < 8, cum_f, cum_b)
        decay_col = jnp.right_shift(lane, 4) == kind
        gate = jnp.where(decay_col, cum, gate)
        gt_ref[...] = gate
        gc_ref[...] = gate.T
    else:
        cos_ref, sin_ref, q_ref, k_ref, v_ref = rest
        cos = cos_ref[...]
        sin = sin_ref[...]
        lane = lax.broadcasted_iota(jnp.int32, cos.shape, 1)
        first_half = (lane % 32) < 16
        nq = q_ref.shape[-1] // LANES
        nk = k_ref.shape[-1] // LANES
        for s in range(nq):
            q_ref[:, s * LANES:(s + 1) * LANES] = (
                _rope_slab(p[:, s * LANES:(s + 1) * LANES], cos, sin, first_half) * Q_SCALE[kind]).astype(BF16)
        for s in range(nk):
            o = (nq + s) * LANES
            k_ref[:, s * LANES:(s + 1) * LANES] = _rope_slab(p[:, o:o + LANES], cos, sin, first_half).astype(BF16)
        o = (nq + nk) * LANES
        v_ref[...] = p[:, o:].astype(BF16)


def _inproj(kind, xt, g, mods, w_main, extra, nct):
    fused = isinstance(xt, tuple)
    nb, t_all, d = (xt[3] if fused else xt).shape
    nt = t_all // TM
    pdim = w_main.shape[1]
    tile = lambda w: pl.BlockSpec((None, TM, w), lambda b, t: (b, t, 0))
    full = lambda a: pl.BlockSpec(a.shape, lambda b, t: (0,) * a.ndim)
    if fused:
        ys, pos_flat, rinfo, x1, mods_prev = xt
        smem = lambda fn: pl.BlockSpec((TOP_K * TM,), fn, memory_space=pltpu.SMEM)
        in_specs = [smem(lambda b, t: (b * nt + t,)), smem(lambda b, t: (jnp.minimum(b * nt + t + 1, nb * nt - 1),)),
                    tile(LANES), tile(d), full(mods_prev), pl.BlockSpec(memory_space=pl.ANY)]
        args = (pos_flat, pos_flat, rinfo, x1, mods_prev, ys)
        scratch = [pltpu.VMEM((2, TOP_K, TM, d), F32), pltpu.SemaphoreType.DMA((2,))]
    else:
        in_specs, args, scratch = [tile(d)], (xt,), []
    in_specs += [full(g), full(mods), full(w_main)]
    args += (g, mods, w_main)
    if kind in (0, 1):
        wg, c0, c1 = extra
        in_specs += [full(wg), full(c0), full(c1)]
        out_shape = (jax.ShapeDtypeStruct((nb, t_all, pdim), F32),
                     jax.ShapeDtypeStruct((nb, t_all, LANES), F32),
                     jax.ShapeDtypeStruct((nb, LANES, t_all), F32))
        out_specs = (tile(pdim), tile(LANES), pl.BlockSpec((None, LANES, TM), lambda b, t: (b, 0, t)))
        args += (wg, c0, c1)
    else:
        cos, sin, widths = extra
        in_specs += [pl.BlockSpec((TM, LANES), lambda b, t: (t, 0))] * 2
        out_shape = tuple(jax.ShapeDtypeStruct((nb, t_all, w), BF16) for w in widths)
        out_specs = tuple(tile(w) for w in widths)
        args += (cos, sin)
    if fused:
        out_shape += (jax.ShapeDtypeStruct((nb, t_all, d), F32),)
        out_specs += (tile(d),)
    return pl.pallas_call(
        functools.partial(_inproj_kernel, kind, nct, nb, d, fused),
        out_shape=out_shape, grid=(nb, nt), in_specs=in_specs, out_specs=out_specs, scratch_shapes=scratch,
        compiler_params=_cp(("arbitrary", "arbitrary")), name=f"inproj{kind}",
    )(*args)


def _inproj_stream(kind, xt, *args):
    outs = _inproj(kind, xt, *args)
    return (outs[:-1], outs[-1]) if isinstance(xt, tuple) else (outs, xt)


def _gdn_prep_kernel(nct, nt, p_ref, prev_ref, next_ref, w_ref, q_ref, k_ref, v_ref, win_ref):
    t = pl.program_id(1)
    prev_ok = jnp.logical_and(t != 0, t != nct)
    next_ok = jnp.logical_and(t != nct - 1, t != nt - 1)
    win_ref[0:8, :] = jnp.where(prev_ok, prev_ref[...], 0.0)
    win_ref[8:8 + TM, :] = p_ref[...]
    win_ref[8 + TM:, :] = jnp.where(next_ok, next_ref[...], 0.0)
    width = q_ref.shape[-1]
    for part, o_ref in enumerate((q_ref, k_ref, v_ref)):
        for h in range(width // LANES):
            c0 = part * width + h * LANES
            acc = None
            for j in range(5):
                term = win_ref[6 + j:6 + j + TM, c0:c0 + LANES] * w_ref[j:j + 1, c0:c0 + LANES]
                acc = term if acc is None else acc + term
            y = acc * _sigmoid(acc)
            if part < 2:
                y = y * lax.rsqrt(jnp.sum(y * y, axis=-1, keepdims=True) + NORM_EPS)
            if part == 0:
                y = y * (LANES ** -0.5)
            o_ref[:, h * LANES:(h + 1) * LANES] = y


def _gdn_prep(p_main, conv_w8, nct):
    nb, t_all, _ = p_main.shape
    nt = t_all // TM
    width = conv_w8.shape[1] // 3
    cw = 3 * width
    t8 = t_all // 8
    r8 = TM // 8
    tile = pl.BlockSpec((None, TM, width), lambda b, t: (b, t, 0))
    return pl.pallas_call(
        functools.partial(_gdn_prep_kernel, nct, nt),
        out_shape=tuple(jax.ShapeDtypeStruct((nb, t_all, width), F32) for _ in range(3)),
        grid=(nb, nt),
        in_specs=[pl.BlockSpec((None, TM, cw), lambda b, t: (b, t, 0)),
                  pl.BlockSpec((None, 8, cw), lambda b, t: (b, jnp.maximum(t * r8 - 1, 0), 0)),
                  pl.BlockSpec((None, 8, cw), lambda b, t: (b, jnp.minimum((t + 1) * r8, t8 - 1), 0)),
                  pl.BlockSpec(conv_w8.shape, lambda b, t: (0, 0))],
        out_specs=(tile, tile, tile),
        scratch_shapes=[pltpu.VMEM((TM + 16, cw), F32)],
        compiler_params=_cp(("arbitrary", "arbitrary")), name="gdn_prep",
    )(p_main, p_main, p_main, conv_w8)


def _split_bf16(a):
    hi = a.astype(BF16)
    return hi, (a - hi.astype(F32)).astype(BF16)


def _dot_mask(mask01, x):
    m = mask01.astype(BF16)
    x1 = x.astype(BF16)
    r1 = x - x1.astype(F32)
    x2 = r1.astype(BF16)
    x3 = (r1 - x2.astype(F32)).astype(BF16)
    d = lambda y: jnp.dot(m, y, preferred_element_type=F32)
    return d(x1) + (d(x2) + d(x3))


def _dot3(a, b):
    ah, al = _split_bf16(a)
    bh, bl = _split_bf16(b)
    d = lambda x, y: jnp.dot(x, y, preferred_element_type=F32)
    return d(ah, bh) + (d(ah, bl) + d(al, bh))


def _unit_triangular_inverses(mats, same):
    a8 = [jnp.where(same(3), a, 0.0) for a in mats]
    ts = [jnp.where(same(0), 1.0, 0.0) - x for x in a8]
    p2 = [_dot3(x, x) for x in a8]
    ts = [t + _dot3(t, p) for t, p in zip(ts, p2)]
    p4 = [_dot3(p, p) for p in p2]
    ts = [t + _dot3(t, p) for t, p in zip(ts, p4)]
    for sh in (4, 5, 6):
        offs = [jnp.where(same(sh), jnp.where(same(sh - 1), 0.0, a), 0.0) for a in mats]
        mids = [_dot3(o, t) for o, t in zip(offs, ts)]
        ts = [t - _dot3(t, m) for t, m in zip(ts, mids)]
    return ts


def _gdn_tiles(probs):
    ii = lax.broadcasted_iota(jnp.int32, (TM, TM), 0)
    jj = lax.broadcasted_iota(jnp.int32, (TM, TM), 1)
    same = lambda sh: jnp.right_shift(ii, sh) == jnp.right_shift(jj, sh)
    incl = {rev: jnp.where(same(CHUNK_SHIFT), jnp.where((ii <= jj) if rev else (ii >= jj), 1.0, 0.0), 0.0) > 0.5
            for rev in {p[8] for p in probs}}
    gams = [jnp.where(incl[rev], jnp.exp(jnp.where(incl[rev], gc - gr, 0.0)), 0.0)
            for (_, _, _, _, gc, gr, _, _, rev) in probs]
    kbs = [p[1] * p[3] for p in probs]
    mats = [jnp.where(same(0), 0.0, _dot_nt(kb, p[1]) * gam) for kb, p, gam in zip(kbs, probs, gams)]
    tinvs = _unit_triangular_inverses(mats, same)
    egs = [jnp.exp(p[4]) for p in probs]
    uws = [_dot3(t, jnp.concatenate([p[2] * p[3], kb * eg], axis=1)) for t, p, kb, eg in zip(tinvs, probs, kbs, egs)]
    qks = [_dot_nt(p[0], p[1]) * gam for p, gam in zip(probs, gams)]
    qgs = [p[0] * eg for p, eg in zip(probs, egs)]
    states = [p[6] for p in probs]
    nchunk = TM // CHUNK
    for n in range(nchunk):
        cs = [nchunk - 1 - n if p[8] else n for p in probs]
        sls = [slice(c * CHUNK, (c + 1) * CHUNK) for c in cs]
        gtots = [p[4][(c * CHUNK if p[8] else (c + 1) * CHUNK - 1):][:1] for p, c in zip(probs, cs)]
        wss = [_dot(uw[sl, LANES:], s) for uw, sl, s in zip(uws, sls, states)]
        qss = [_dot(qg[sl], s) for qg, sl, s in zip(qgs, sls, states)]
        vns = [uw[sl, :LANES] - ws for uw, sl, ws in zip(uws, sls, wss)]
        for p, sl, qs, qk, vn in zip(probs, sls, qss, qks, vns):
            p[7](sl, qs + _dot(qk[sl, sl], vn))
        states = [s * jnp.exp(gt) + _dot_tn(p[1][sl] * jnp.exp(gt - p[4][sl]), vn)
                  for s, gt, p, sl, vn in zip(states, gtots, probs, sls, vns)]
    return states


def _gate_col(gt, idx):
    lane = lax.broadcasted_iota(jnp.int32, gt.shape, 1)
    return jnp.sum(jnp.where(lane == idx, gt, 0.0), axis=1, keepdims=True)


def _gate_row(gc, idx):
    sub = lax.broadcasted_iota(jnp.int32, gc.shape, 0)
    return jnp.sum(jnp.where(sub == idx, gc, 0.0), axis=0, keepdims=True)


def _gdn_scan_kernel(qf, kf, vf, gtf, gcf, qb, kb, vb, gtb, gcb, of_ref, ob_ref, s_ref):
    hp = pl.program_id(1)

    @pl.when(pl.program_id(2) == 0)
    def _():
        s_ref[...] = jnp.zeros_like(s_ref)

    probs = []
    for rev, (q_r, k_r, v_r, gt_r, gc_r, o_r) in enumerate(
            ((qf, kf, vf, gtf, gcf, of_ref), (qb, kb, vb, gtb, gcb, ob_ref))):
        gt = gt_r[...]
        gc = gc_r[...]
        for j in range(GDN_HEADS_PER_STEP):
            head = hp * GDN_HEADS_PER_STEP + j
            cols = slice(j * LANES, (j + 1) * LANES)

            def store(sl, val, o_r=o_r, cols=cols):
                o_r[sl, cols] = val

            probs.append((q_r[:, cols], k_r[:, cols], v_r[:, cols], _gate_col(gt, 16 + rev * 8 + head),
                          _gate_col(gt, rev * 8 + head), _gate_row(gc, rev * 8 + head), s_ref[rev, j], store,
                          bool(rev)))
    states = _gdn_tiles(probs)
    for rev in range(2):
        for j in range(GDN_HEADS_PER_STEP):
            s_ref[rev, j] = states[rev * GDN_HEADS_PER_STEP + j]


def _scan_tiles(nct, nt):
    fwd = lambda s: s
    bwd = lambda s: jnp.where(s < nct, nct - 1 - s, nt - 1 - (s - nct))
    return fwd, bwd


def _gdn_scan(q, k, v, gt, gc, nct):
    nb, t_all, width = q.shape
    hw = GDN_HEADS_PER_STEP * LANES
    nt = t_all // TM
    fwd, bwd = _scan_tiles(nct, nt)

    def specs(tile_of):
        head = pl.BlockSpec((None, TM, hw), lambda b, h, s: (b, tile_of(s), h))
        return [head, head, head,
                pl.BlockSpec((None, TM, LANES), lambda b, h, s: (b, tile_of(s), 0)),
                pl.BlockSpec((None, LANES, TM), lambda b, h, s: (b, 0, tile_of(s)))]

    out = jax.ShapeDtypeStruct((nb, t_all, width), F32)
    return pl.pallas_call(
        _gdn_scan_kernel, out_shape=(out, out), grid=(nb, width // hw, nt),
        in_specs=specs(fwd) + specs(bwd),
        out_specs=(pl.BlockSpec((None, TM, hw), lambda b, h, s: (b, fwd(s), h)),
                   pl.BlockSpec((None, TM, hw), lambda b, h, s: (b, bwd(s), h))),
        scratch_shapes=[pltpu.VMEM((2, GDN_HEADS_PER_STEP, LANES, LANES), F32)],
        compiler_params=_cp(("arbitrary", "arbitrary", "arbitrary")), name="gdn_scan",
    )(q, k, v, gt, gc, q, k, v, gt, gc)


def _mlstm_tiles(probs):
    ii = lax.broadcasted_iota(jnp.int32, (TM, TM), 0)
    jj = lax.broadcasted_iota(jnp.int32, (TM, TM), 1)
    same_chunk = jnp.right_shift(ii, CHUNK_SHIFT) == jnp.right_shift(jj, CHUNK_SHIFT)
    incl = {rev: jnp.where(same_chunk, jnp.where((ii <= jj) if rev else (ii >= jj), 1.0, 0.0), 0.0) > 0.5
            for rev in {p[10] for p in probs}}
    nchunk = TM // CHUNK
    order = lambda p: [nchunk - 1 - n if p[10] else n for n in range(nchunk)]
    rows = lambda c: slice(c * CHUNK, (c + 1) * CHUNK)
    dlogs = [jnp.where(incl[p[10]], p[5] - p[6] + p[4], -jnp.inf) for p in probs]
    m_intras = [jnp.max(d, axis=-1, keepdims=True) for d in dlogs]
    qks = [_dot_nt(p[0], p[1]) for p in probs]
    btots, m_locs, c_locs = [], [], []
    for p in probs:
        bt, ml, cl = {}, {}, {}
        for c in order(p):
            last = c * CHUNK if p[10] else (c + 1) * CHUNK - 1
            bt[c] = p[5][last:last + 1]
            a_loc = bt[c] - p[5][rows(c)] + p[3][rows(c)]
            ml[c] = jnp.max(a_loc, axis=0, keepdims=True)
            cl[c] = _dot_tn(p[1][rows(c)] * jnp.exp(a_loc - ml[c]), p[2][rows(c)])
        btots.append(bt)
        m_locs.append(ml)
        c_locs.append(cl)
    new_states, c_ins, m_ins = [], [], []
    for p, bt, ml, cl in zip(probs, btots, m_locs, c_locs):
        c_st, m_st = p[7], p[8]
        ci, mi = {}, {}
        for c in order(p):
            ci[c], mi[c] = c_st, m_st
            m_new = jnp.maximum(bt[c] + m_st, ml[c])
            c_st = c_st * jnp.exp(bt[c] + m_st - m_new) + cl[c] * jnp.exp(ml[c] - m_new)
            m_st = m_new
        new_states.append((c_st, m_st))
        c_ins.append(ci)
        m_ins.append(mi)
    inters = [jnp.concatenate([_dot(p[0][rows(c)], ci[c]) for c in range(nchunk)], axis=0)
              for p, ci in zip(probs, c_ins)]
    for p, dlog, m_intra, qk, inter, mi in zip(probs, dlogs, m_intras, qks, inters, m_ins):
        m_in = jnp.concatenate([jnp.broadcast_to(mi[c], (CHUNK, 1)) for c in range(nchunk)], axis=0)
        log_inter = p[5] + m_in
        m_t = jnp.maximum(log_inter, m_intra)
        num = jnp.exp(log_inter - m_t) * inter + _dot(jnp.exp(dlog - m_t) * qk, p[2])
        dv = p[2].shape[1] // 2
        p[9](num[:, :dv] / jnp.maximum(jnp.abs(num[:, dv:dv + 1]), jnp.exp(-m_t)))
    return new_states


def _mlstm_scan_kernel(qf, kf, vf, gtf, gcf, qb, kb, vb, gtb, gcb, of_ref, ob_ref, c_ref, m_ref):
    hp = pl.program_id(1)

    @pl.when(pl.program_id(2) == 0)
    def _():
        c_ref[...] = jnp.zeros_like(c_ref)
        m_ref[...] = jnp.zeros_like(m_ref)

    dqk = qf.shape[-1] // 2
    dv = vf.shape[-1] // 2
    lane = lax.broadcasted_iota(jnp.int32, (TM, dv), 1)
    ones_pad = jnp.where(lane == 0, 1.0, 0.0)
    probs = []
    for rev, (q_r, k_r, v_r, gt_r, gc_r, o_r) in enumerate(
            ((qf, kf, vf, gtf, gcf, of_ref), (qb, kb, vb, gtb, gcb, ob_ref))):
        gt = gt_r[...]
        gc = gc_r[...]
        for j in range(2):
            head = 2 * hp + j

            def store(val, o_r=o_r, j=j):
                o_r[:, j * dv:(j + 1) * dv] = val

            probs.append((q_r[:, j * dqk:(j + 1) * dqk], k_r[:, j * dqk:(j + 1) * dqk] * (dqk ** -0.5),
                          jnp.concatenate([v_r[:, j * dv:(j + 1) * dv], ones_pad], axis=1),
                          _gate_col(gt, rev * 8 + head), _gate_row(gc, rev * 8 + head),
                          _gate_col(gt, 16 + rev * 8 + head), _gate_row(gc, 16 + rev * 8 + head),
                          c_ref[rev, j], m_ref[rev, j][0:1, 0:1], store, bool(rev)))
    states = _mlstm_tiles(probs)
    for rev in range(2):
        for j in range(2):
            c_st, m_st = states[rev * 2 + j]
            c_ref[rev, j] = c_st
            m_ref[rev, j] = jnp.broadcast_to(m_st, (8, LANES))


def _mlstm_scan(p, qw, vw, gt, gc, nct):
    nb, t_all, _ = p.shape
    nhp = qw // LANES
    nt = t_all // TM
    dqk = LANES // 2
    dv = vw // nhp // 2
    fwd, bwd = _scan_tiles(nct, nt)

    def specs(tile_of):
        return [pl.BlockSpec((None, TM, LANES), lambda b, h, s: (b, tile_of(s), h)),
                pl.BlockSpec((None, TM, LANES), lambda b, h, s: (b, tile_of(s), nhp + h)),
                pl.BlockSpec((None, TM, 2 * dv), lambda b, h, s: (b, tile_of(s), 2 * qw // (2 * dv) + h)),
                pl.BlockSpec((None, TM, LANES), lambda b, h, s: (b, tile_of(s), 0)),
                pl.BlockSpec((None, LANES, TM), lambda b, h, s: (b, 0, tile_of(s)))]

    out = jax.ShapeDtypeStruct((nb, t_all, vw), F32)
    return pl.pallas_call(
        _mlstm_scan_kernel, out_shape=(out, out), grid=(nb, nhp, nt),
        in_specs=specs(fwd) + specs(bwd),
        out_specs=(pl.BlockSpec((None, TM, 2 * dv), lambda b, h, s: (b, fwd(s), h)),
                   pl.BlockSpec((None, TM, 2 * dv), lambda b, h, s: (b, bwd(s), h))),
        scratch_shapes=[pltpu.VMEM((2, 2, dqk, 2 * dv), F32), pltpu.VMEM((2, 2, 8, LANES), F32)],
        compiler_params=_cp(("arbitrary", "arbitrary", "arbitrary")), name="mlstm_scan",
    )(p, p, p, gt, gc, p, p, p, gt, gc)


def _diff_attn_kernel(nk, lambda_init, lam_ref, q_ref, k_ref, v_ref, o_ref, m_ref, acc_ref):
    j = pl.program_id(3)
    dh = LANES // 2

    @pl.when(j == 0)
    def _():
        m_ref[...] = jnp.full(m_ref.shape, -jnp.inf, F32)
        acc_ref[...] = jnp.zeros_like(acc_ref)

    tq = q_ref.shape[0]
    tk = k_ref.shape[0]
    ones_col = jnp.where(lax.broadcasted_iota(jnp.int32, (tk, LANES), 1) == 0, 1.0, 0.0).astype(BF16)
    v_aug = jnp.concatenate([v_ref[...], ones_col], axis=1)
    rs = DIFF_ROWS if tq % DIFF_ROWS == 0 else tq
    probs = [(hh, slice(r * rs, (r + 1) * rs)) for r in range(tq // rs) for hh in range(2)]

    def score(prob):
        hh, rows = prob
        return lax.dot_general(q_ref[rows, hh * dh:(hh + 1) * dh], k_ref[:, hh * dh:(hh + 1) * dh],
                               (((1,), (1,)), ((), ())), preferred_element_type=F32)

    scores = [score(p) for p in probs[:DIFF_QK_AHEAD]]
    blocks = [slice(c * LANES, (c + 1) * LANES) for c in range(tk // LANES)]
    for n, (hh, rows) in enumerate(probs):
        s = scores[n]
        if n + DIFF_QK_AHEAD < len(probs):
            scores.append(score(probs[n + DIFF_QK_AHEAD]))
        part = s[:, blocks[0]]
        for blk in blocks[1:]:
            part = jnp.maximum(part, s[:, blk])
        m_old = m_ref[hh, rows, :]
        m_new = jnp.maximum(m_old, jnp.broadcast_to(jnp.max(part, axis=-1, keepdims=True), (rs, LANES)))
        alpha = jnp.exp2(m_old - m_new)
        p = jnp.concatenate([jnp.exp2(s[:, blk] - m_new).astype(BF16) for blk in blocks], axis=1)
        acc_ref[hh, rows, :] = (jnp.concatenate([alpha, alpha], axis=1) * acc_ref[hh, rows, :]
                                + jnp.dot(p, v_aug, preferred_element_type=F32))
        m_ref[hh, rows, :] = m_new

    @pl.when(j == nk - 1)
    def _():
        lp = lam_ref[...]
        lam = (jnp.exp(jnp.sum(lp[0:1] * lp[1:2], axis=-1, keepdims=True))
               - jnp.exp(jnp.sum(lp[2:3] * lp[3:4], axis=-1, keepdims=True)) + lambda_init)
        a0 = acc_ref[0]
        a1 = acc_ref[1]
        o_ref[...] = a0[:, :LANES] / a0[:, LANES:LANES + 1] - lam * (a1[:, :LANES] / a1[:, LANES:LANES + 1])


def _pick_tile(n, candidates):
    return next(c for c in candidates if n % c == 0)


def _diff_attn(q, k, v, lam_p, lambda_init):
    nb, nq_rows, width = q.shape
    ns = k.shape[1]
    tq = _pick_tile(nq_rows, (2048, 1024, 512, 256))
    tk = _pick_tile(ns, (1280, 1024, 768, 512, 256))
    nk = ns // tk
    kv = pl.BlockSpec((None, tk, LANES), lambda b, h, i, j: (b, j, h))
    qo = pl.BlockSpec((None, tq, LANES), lambda b, h, i, j: (b, i, h))
    return pl.pallas_call(
        functools.partial(_diff_attn_kernel, nk, lambda_init),
        out_shape=jax.ShapeDtypeStruct((nb, nq_rows, width), F32),
        grid=(nb, width // LANES, nq_rows // tq, nk),
        in_specs=[pl.BlockSpec(lam_p.shape, lambda b, h, i, j: (0, 0)), qo, kv, kv],
        out_specs=qo,
        scratch_shapes=[pltpu.VMEM((2, tq, LANES), F32), pltpu.VMEM((2, tq, 2 * LANES), F32)],
        compiler_params=_cp(("arbitrary", "arbitrary", "arbitrary", "arbitrary")), name="diff_attn",
    )(lam_p, q, k, v)


def _swa_kernel(nct, nt, seq, sink_ref, q_ref, kc_ref, kp_ref, k0_ref, kn_ref, vc_ref, vp_ref, v0_ref, vn_ref,
                o_ref):
    i = pl.program_id(1)
    dh = LANES // 2
    nkv = kc_ref.shape[-1] // dh
    group = (q_ref.shape[-1] // dh) // nkv
    lc = kc_ref.shape[0]
    col = lax.broadcasted_iota(jnp.int32, (TM, lc + 3 * TM), 1)
    qpos = (i - nct) * TM + lax.broadcasted_iota(jnp.int32, (TM, lc + 3 * TM), 0)
    kpos = (i - nct - 1) * TM + col - lc
    ninf = -jnp.inf
    in_window = jnp.where(jnp.abs(qpos - kpos) <= SWA_WINDOW,
                          jnp.where(kpos >= 0, jnp.where(kpos < seq, 0.0, ninf), ninf), ninf)
    bias = jnp.where(col < lc, 0.0, jnp.where(i >= nct, in_window, ninf))
    kcat = jnp.concatenate([kc_ref[...], kp_ref[...], k0_ref[...], kn_ref[...]], axis=0)
    vcat = jnp.concatenate([vc_ref[...], vp_ref[...], v0_ref[...], vn_ref[...]], axis=0)
    sink = sink_ref[...]
    scores = [lax.dot_general(q_ref[:, hq * dh:(hq + 1) * dh], kcat[:, (hq // group) * dh:(hq // group + 1) * dh],
                              (((1,), (1,)), ((), ())), preferred_element_type=F32) for hq in range(nkv * group)]
    nkeys = lc + 3 * TM
    ones_col = jnp.where(lax.broadcasted_iota(jnp.int32, (nkeys, dh), 1) == 0, 1.0, 0.0).astype(BF16)
    v_aug = [jnp.concatenate([vcat[:, g * dh:(g + 1) * dh], ones_col], axis=1) for g in range(nkv)]
    blocks = [slice(c * LANES, (c + 1) * LANES) for c in range(nkeys // LANES)]
    biases = [bias[:, blk] for blk in blocks]
    for hq, s in enumerate(scores):
        sblk = [s[:, blk] + b for blk, b in zip(blocks, biases)]
        part = sblk[0]
        for x in sblk[1:]:
            part = jnp.maximum(part, x)
        sb = jnp.broadcast_to(sink[0:1, hq:hq + 1], (TM, LANES))
        m = jnp.maximum(jnp.broadcast_to(jnp.max(part, axis=-1, keepdims=True), (TM, LANES)), sb)
        e = jnp.concatenate([jnp.exp(x - m).astype(BF16) for x in sblk], axis=1)
        ov = jnp.dot(e, v_aug[hq // group], preferred_element_type=F32)
        denom = ov[:, dh:dh + 1] + jnp.exp(sb - m)[:, 0:1]
        o_ref[:, hq * dh:(hq + 1) * dh] = ov[:, :dh] / denom


def _swa_attn(q, k, v, sink_row, nct, seq):
    nb, t_all, qw = q.shape
    kw = k.shape[-1]
    nt = t_all // TM
    lc = nct * TM
    ctxs = pl.BlockSpec((None, lc, kw), lambda b, i: (b, 0, 0))
    prev = pl.BlockSpec((None, TM, kw), lambda b, i: (b, jnp.maximum(i - 1, 0), 0))
    cur = pl.BlockSpec((None, TM, kw), lambda b, i: (b, i, 0))
    nxt = pl.BlockSpec((None, TM, kw), lambda b, i: (b, jnp.minimum(i + 1, nt - 1), 0))
    qo = pl.BlockSpec((None, TM, qw), lambda b, i: (b, i, 0))
    return pl.pallas_call(
        functools.partial(_swa_kernel, nct, nt, seq),
        out_shape=jax.ShapeDtypeStruct((nb, t_all, qw), F32),
        grid=(nb, nt),
        in_specs=[pl.BlockSpec(sink_row.shape, lambda b, i: (0, 0)), qo, ctxs, prev, cur, nxt, ctxs, prev, cur, nxt],
        out_specs=qo,
        compiler_params=_cp(("arbitrary", "arbitrary")), name="swa_attn",
    )(sink_row, q, k, k, k, k, v, v, v, v)


def _headnorm(o, g_row):
    outs = []
    for h in range(o.shape[-1] // LANES):
        seg = o[:, h * LANES:(h + 1) * LANES]
        ms = jnp.mean(seg * seg, axis=-1, keepdims=True)
        outs.append(seg * lax.rsqrt(ms + NORM_EPS) * g_row)
    return jnp.concatenate(outs, axis=1)


def _outproj_kernel(kind, nct, nb, d, out_scale, *refs):
    b = pl.program_id(0)
    t = pl.program_id(1)
    row = jnp.where(t < nct, nb, b)
    if kind in (0, 1):
        of_ref, ob_ref, z_ref, hg_ref = refs[:4]
        rest = refs[4:]
        z = z_ref[...]
        o = _headnorm(of_ref[...] + ob_ref[...], hg_ref[...]) * (z * _sigmoid(z) if kind == 0 else _sigmoid(z))
    elif kind == 2:
        o_ref, hg_ref = refs[:2]
        rest = refs[2:]
        o = _headnorm(o_ref[...], hg_ref[...]) * out_scale
    else:
        o_ref = refs[0]
        rest = refs[1:]
        o = o_ref[...]
    x_ref, m_ref, w_ref, gf_ref, rw_ref, rb_ref, x1_ref, f_ref, ri_ref, cnt_ref, carry_ref = rest

    @pl.when(jnp.logical_and(b == 0, t == 0))
    def _():
        carry_ref[...] = jnp.zeros_like(carry_ref)

    x1 = x_ref[...] + _mod_row(m_ref, row, 2, d) * _dot(o, w_ref[...])
    x1_ref[...] = x1
    f = _prenorm_mod(x1, gf_ref[...], _mod_row(m_ref, row, 3, d), _mod_row(m_ref, row, 4, d))
    f_ref[...] = f
    logits = _dot3(f, rw_ref[...]) + rb_ref[...]
    lane = lax.broadcasted_iota(jnp.int32, logits.shape, 1)
    vals, idxs, sels = [], [], []
    cur = logits
    for _ in range(TOP_K):
        m = jnp.max(cur, axis=-1, keepdims=True)
        idx = jnp.min(jnp.where(cur == m, lane, LANES), axis=-1, keepdims=True)
        sel = lane == idx
        cur = jnp.where(sel, -jnp.inf, cur)
        vals.append(m)
        idxs.append(idx)
        sels.append(sel)
    es = [jnp.exp(v - vals[0]) for v in vals]
    esum = es[0] + es[1] + es[2] + es[3]
    assign = jnp.where(sels[0] | sels[1] | sels[2] | sels[3], 1.0, 0.0)
    ii = lax.broadcasted_iota(jnp.int32, (TM, TM), 0)
    jj = lax.broadcasted_iota(jnp.int32, (TM, TM), 1)
    before = _dot(jnp.where(ii > jj, 1.0, 0.0), assign) + carry_ref[...]
    info = jnp.zeros(logits.shape, F32)
    for k in range(TOP_K):
        rank = jnp.sum(jnp.where(sels[k], before, 0.0), axis=-1, keepdims=True)
        info = jnp.where(lane == k, idxs[k].astype(F32), info)
        info = jnp.where(lane == TOP_K + k, rank, info)
        info = jnp.where(lane == 2 * TOP_K + k, es[k] / esum, info)
    ri_ref[...] = info
    carry_ref[...] = carry_ref[...] + jnp.sum(assign, axis=0, keepdims=True)
    cnt_ref[...] = carry_ref[...]


def _outproj(kind, mixer_out, xt, mods, w_out, gf, rw, rb, nct, out_scale=1.0):
    nb, t_all, d = xt.shape
    nt = t_all // TM
    tile = lambda w: pl.BlockSpec((None, TM, w), lambda b, t: (b, t, 0))
    full = lambda a: pl.BlockSpec(a.shape, lambda b, t: (0,) * a.ndim)
    if kind in (0, 1):
        of, ob, pm, zoff, hg = mixer_out
        zblk = zoff // d
        lead = [of, ob, pm, hg]
        lead_specs = [tile(d), tile(d), pl.BlockSpec((None, TM, d), lambda b, t: (b, t, zblk)), full(hg)]
    elif kind == 2:
        o, hg = mixer_out
        lead = [o, hg]
        lead_specs = [tile(d), full(hg)]
    else:
        lead = [mixer_out]
        lead_specs = [tile(d)]
    args = lead + [xt, mods, w_out, gf, rw, rb]
    in_specs = lead_specs + [tile(d), full(mods), full(w_out), full(gf), full(rw), full(rb)]
    return pl.pallas_call(
        functools.partial(_outproj_kernel, kind, nct, nb, d, out_scale),
        out_shape=(jax.ShapeDtypeStruct((nb, t_all, d), F32), jax.ShapeDtypeStruct((nb, t_all, d), F32),
                   jax.ShapeDtypeStruct((nb, t_all, LANES), F32), jax.ShapeDtypeStruct((1, LANES), F32)),
        grid=(nb, nt), in_specs=in_specs,
        out_specs=(tile(d), tile(d), tile(LANES), pl.BlockSpec((1, LANES), lambda b, t: (0, 0))),
        scratch_shapes=[pltpu.VMEM((1, LANES), F32)],
        compiler_params=_cp(("arbitrary", "arbitrary")), name=f"outproj{kind}",
    )(*args)


def _dispatch_kernel(zstart_ref, nused_ref, pos_ref, f_ref, xs_out, zeros, sem, zsem):
    @pl.when(pl.program_id(0) == 0)
    def _():
        zeros[...] = jnp.zeros_like(zeros)

        def zero_copy(row0):
            return pltpu.make_async_copy(zeros, xs_out.at[pl.ds(pl.multiple_of(row0, MOE_BM), MOE_BM)], zsem)

        n_blk = xs_out.shape[0] // MOE_BM
        for e in range(zstart_ref.shape[0]):
            pl.when(zstart_ref[e] >= 0)(lambda e=e: zero_copy(zstart_ref[e]).start())
        lax.fori_loop(nused_ref[0], n_blk, lambda b, c: (zero_copy(b * MOE_BM).start(), c)[1], 0)
        for e in range(zstart_ref.shape[0]):
            pl.when(zstart_ref[e] >= 0)(lambda e=e: zero_copy(zstart_ref[e]).wait())
        lax.fori_loop(nused_ref[0], n_blk, lambda b, c: (zero_copy(b * MOE_BM).wait(), c)[1], 0)

    def row_copy(r, k):
        return pltpu.make_async_copy(f_ref.at[pl.ds(r, 1)], xs_out.at[pl.ds(pos_ref[TOP_K * r + k], 1)], sem)

    def start(r, c):
        for k in range(TOP_K):
            row_copy(r, k).start(priority=k % 2)
        return c

    def wait(r, c):
        for k in range(TOP_K):
            row_copy(r, k).wait()
        return c

    lax.fori_loop(0, TM, start, 0)
    lax.fori_loop(0, TM, wait, 0)


def _dispatch(f2, pos_flat, zstart, n_used, n_rows):
    n, d = f2.shape
    grid_spec = pltpu.PrefetchScalarGridSpec(
        num_scalar_prefetch=2, grid=(n // TM,),
        in_specs=[pl.BlockSpec((TOP_K * TM,), lambda i, z, u: (i,), memory_space=pltpu.SMEM),
                  pl.BlockSpec((TM, d), lambda i, z, u: (i, 0))],
        out_specs=pl.BlockSpec(memory_space=pl.ANY),
        scratch_shapes=[pltpu.VMEM((MOE_BM, d), F32), pltpu.SemaphoreType.DMA, pltpu.SemaphoreType.DMA])
    return pl.pallas_call(
        _dispatch_kernel, out_shape=jax.ShapeDtypeStruct((n_rows, d), F32), grid_spec=grid_spec,
        compiler_params=_cp(("arbitrary",)), name="moe_dispatch",
    )(zstart, n_used, pos_flat, f2)


def _expert_kernel(be_ref, nused_ref, xs_ref, wgu_ref, bgu_ref, wd_ref, bd_ref, ys_ref, wgu_bf, wd_bf):
    i = pl.program_id(0)
    ff = wd_ref.shape[0]

    @pl.when(i >= nused_ref[0])
    def _():
        ys_ref[...] = jnp.zeros_like(ys_ref)

    @pl.when(i < nused_ref[0])
    def _():
        @pl.when(jnp.logical_or(i == 0, be_ref[i] != be_ref[jnp.maximum(i - 1, 0)]))
        def _():
            wgu_bf[...] = wgu_ref[...].astype(BF16)
            wd_bf[...] = wd_ref[...].astype(BF16)

        gu = _dot(xs_ref[...], wgu_bf[...]) + bgu_ref[...]
        x_glu = jnp.minimum(gu[:, :ff], SWIGLU_LIMIT)
        x_lin = jnp.clip(gu[:, ff:], -SWIGLU_LIMIT, SWIGLU_LIMIT)
        act = x_glu * _sigmoid(SWIGLU_ALPHA * x_glu) * (x_lin + 1.0)
        ys_ref[...] = _dot(act, wd_bf[...]) + bd_ref[...]


def _expert_ffn(xs, blk_exp, n_used, layer, w_gu, b_gu, w_down, b_down):
    n_rows, d = xs.shape
    _, ne, _, f2 = w_gu.shape
    ff = w_down.shape[2]
    grid_spec = pltpu.PrefetchScalarGridSpec(
        num_scalar_prefetch=2, grid=(n_rows // MOE_BM,),
        in_specs=[pl.BlockSpec((MOE_BM, d), lambda i, be, nu: (i, 0)),
                  pl.BlockSpec((None, None, d, f2), lambda i, be, nu: (layer, be[i], 0, 0)),
                  pl.BlockSpec((None, 1, f2), lambda i, be, nu: (be[i], 0, 0)),
                  pl.BlockSpec((None, None, ff, d), lambda i, be, nu: (layer, be[i], 0, 0)),
                  pl.BlockSpec((None, 1, d), lambda i, be, nu: (be[i], 0, 0))],
        out_specs=pl.BlockSpec((MOE_BM, d), lambda i, be, nu: (i, 0)),
        scratch_shapes=[pltpu.VMEM((d, f2), BF16), pltpu.VMEM((ff, d), BF16)])
    return pl.pallas_call(
        _expert_kernel, out_shape=jax.ShapeDtypeStruct((n_rows, d), F32), grid_spec=grid_spec,
        compiler_params=_cp(("arbitrary",)), name="moe_experts",
    )(blk_exp, n_used, xs, w_gu, b_gu.reshape(ne, 1, f2), w_down, b_down.reshape(ne, 1, d))


def _combine_kernel(nct, nb, nt, d, pos_ref, ri_ref, x1_ref, m_ref, ys_hbm, x2_ref, buf, sem):
    i = pl.program_id(0)
    b = i // nt
    t = i % nt
    row = jnp.where(t < nct, nb, b)

    def row_copy(r, k):
        return pltpu.make_async_copy(ys_hbm.at[pl.ds(pos_ref[TOP_K * r + k], 1)], buf.at[k, pl.ds(r, 1)], sem)

    def start(r, c):
        for k in range(TOP_K):
            row_copy(r, k).start(priority=k % 2)
        return c

    def wait(r, c):
        for k in range(TOP_K):
            row_copy(r, k).wait()
        return c

    lax.fori_loop(0, TM, start, 0)
    lax.fori_loop(0, TM, wait, 0)
    ri = ri_ref[...]
    y = None
    for k in range(TOP_K):
        term = buf[k] * ri[:, 2 * TOP_K + k:2 * TOP_K + k + 1]
        y = term if y is None else y + term
    x2_ref[...] = x1_ref[...] + _mod_row(m_ref, row, 5, d) * y


def _combine(ys, pos_flat, rinfo2, x1_2, mods, nct, nb):
    n, d = x1_2.shape
    nt = n // TM // nb
    return pl.pallas_call(
        functools.partial(_combine_kernel, nct, nb, nt, d),
        out_shape=jax.ShapeDtypeStruct((n, d), F32),
        grid=(n // TM,),
        in_specs=[pl.BlockSpec((TOP_K * TM,), lambda i: (i,), memory_space=pltpu.SMEM),
                  pl.BlockSpec((TM, LANES), lambda i: (i, 0)),
                  pl.BlockSpec((TM, d), lambda i: (i, 0)),
                  pl.BlockSpec(mods.shape, lambda i: (0, 0)),
                  pl.BlockSpec(memory_space=pl.ANY)],
        out_specs=pl.BlockSpec((TM, d), lambda i: (i, 0)),
        scratch_shapes=[pltpu.VMEM((TOP_K, TM, d), F32), pltpu.SemaphoreType.DMA],
        compiler_params=_cp(("arbitrary",)), name="moe_combine",
    )(pos_flat, rinfo2, x1_2, mods, ys)


def _moe(f, rinfo, counts, x1, mods, layer, w_gu, b_gu, w_down, b_down, nct, last):
    nb, t_all, d = f.shape
    n = nb * t_all
    ne = w_gu.shape[1]
    ri2 = rinfo.reshape(n, LANES)
    idx = ri2[:, :TOP_K].astype(jnp.int32)
    rank = ri2[:, TOP_K:2 * TOP_K].astype(jnp.int32)
    cnt = counts[0, :ne].astype(jnp.int32)
    padded = (cnt + MOE_BM - 1) // MOE_BM * MOE_BM
    pad_end = jnp.cumsum(padded)
    pad_start = pad_end - padded
    onehot = idx[:, :, None] == jnp.arange(ne, dtype=jnp.int32)[None, None, :]
    pos = (rank + jnp.sum(jnp.where(onehot, pad_start[None, None, :], 0), axis=-1)).reshape(-1)
    n_blk = (n * TOP_K + MOE_BM - 1) // MOE_BM + ne
    n_used = (pad_end[-1:] // MOE_BM).astype(jnp.int32)
    blk_row = jnp.minimum(jnp.arange(n_blk, dtype=jnp.int32), n_used - 1)
    blk_exp = jnp.sum(blk_row[:, None] * MOE_BM >= pad_end[None, :], axis=-1).astype(jnp.int32)
    zstart = jnp.where(padded > 0, pad_end - MOE_BM, -1).astype(jnp.int32)
    xs = _dispatch(f.reshape(n, d), pos, zstart, n_used, n_blk * MOE_BM)
    ys = _expert_ffn(xs, blk_exp, n_used, layer, w_gu, b_gu, w_down, b_down)
    if not last:
        return ys, pos, rinfo, x1, mods
    x2 = _combine(ys, pos, ri2, x1.reshape(n, d), mods, nct, nb)
    return x2.reshape(nb, t_all, d)


def _final_kernel(x_ref, g_ref, o_ref):
    x = x_ref[...]
    o_ref[...] = x * lax.rsqrt(jnp.mean(x * x, axis=-1, keepdims=True) + NORM_EPS) * g_ref[...]


def _final_norm(xt, g, nct):
    nb, t_all, d = xt.shape
    seq = t_all - nct * TM
    return pl.pallas_call(
        _final_kernel, out_shape=jax.ShapeDtypeStruct((nb, seq, d), F32), grid=(nb, seq // TM),
        in_specs=[pl.BlockSpec((None, TM, d), lambda b, t: (b, t + nct, 0)), pl.BlockSpec(g.shape, lambda b, t: (0, 0))],
        out_specs=pl.BlockSpec((None, TM, d), lambda b, t: (b, t, 0)),
        compiler_params=_cp(("arbitrary", "arbitrary")), name="final_norm",
    )(xt, g)


def _rope_tables(seq, lc):
    rows = seq // GRID_W
    row = jnp.repeat(jnp.arange(rows), GRID_W).astype(F32)
    col = jnp.tile(jnp.arange(GRID_W), rows).astype(F32)
    inv = ROPE_BASE ** (-jnp.arange(16, dtype=F32) / 16)
    ang_r = row[:, None] * inv[None, :]
    ang_c = col[:, None] * inv[None, :]
    cos = jnp.concatenate([jnp.cos(ang_r)] * 2 + [jnp.cos(ang_c)] * 2, axis=1)
    sin = jnp.concatenate([-jnp.sin(ang_r), jnp.sin(ang_r), -jnp.sin(ang_c), jnp.sin(ang_c)], axis=1)
    cos = jnp.concatenate([jnp.ones((lc, 64), F32), cos], axis=0)
    sin = jnp.concatenate([jnp.zeros((lc, 64), F32), sin], axis=0)
    return jnp.tile(cos, (1, 2)), jnp.tile(sin, (1, 2))


def _pad_lanes(a, fill=0.0):
    return jnp.pad(a, [(0, 0)] * (a.ndim - 1) + [(0, LANES - a.shape[-1])], constant_values=fill)


def _gate_consts(first, second):
    return _pad_lanes(jnp.concatenate([first.reshape(1, -1), second.reshape(1, -1)], axis=1))


def kernel(x, c, ctx, c_ctx, ada_w, ada_b, norm_mix_g, norm_ffn_g, final_g, gdn_w_in, gdn_conv_w, gdn_a_log, gdn_dt_bias, gdn_norm_g, gdn_w_out, mlstm_w_in, mlstm_b_i, mlstm_b_f, mlstm_norm_g, mlstm_w_out, diff_w_in, diff_lambda, diff_norm_g, diff_w_out, swa_w_in, swa_sink, swa_w_out, router_w, router_b, moe_w_gu, moe_b_gu, moe_w_down, moe_b_down):
    nb, seq, d = x.shape
    lc = ctx.shape[1]
    depth = ada_w.shape[0]
    assert lc % TM == 0 and seq % TM == 0 and nb < 8
    nct = lc // TM
    xt = jnp.concatenate([ctx, x], axis=1)
    cond8 = jnp.zeros((8, d), F32).at[:nb].set(c).at[nb].set(c_ctx)
    mods_all = _ada_mods(cond8, ada_w, ada_b)
    cos, sin = _rope_tables(seq, lc)
    for i in range(depth):
        kind, j = i % 4, i // 4
        mods = mods_all[i]
        gm = norm_mix_g[i].reshape(1, d)
        gf = norm_ffn_g[i].reshape(1, d)
        rw = _pad_lanes(router_w[i])
        rb = _pad_lanes(router_b[i].reshape(1, -1), -jnp.inf)
        if kind == 0:
            w = gdn_w_in[j]
            main = w.shape[1] - 32
            (p, gt, gc), xt = _inproj_stream(
                0, xt, gm, mods, w[:, :main].astype(BF16),
                (_pad_lanes(w[:, main:]), _gate_consts(gdn_a_log[j], jnp.zeros_like(gdn_a_log[j])),
                 _gate_consts(gdn_dt_bias[j], jnp.zeros_like(gdn_dt_bias[j]))), nct)
            q, k, v = _gdn_prep(p, jnp.pad(gdn_conv_w[j], ((0, 3), (0, 0))), nct)
            of, ob = _gdn_scan(q, k, v, gt, gc, nct)
            mixer_out = (of, ob, p, 3 * d, gdn_norm_g[j].reshape(1, -1))
            w_out, scale = gdn_w_out[j], 1.0
        elif kind == 1:
            w = mlstm_w_in[j]
            main = w.shape[1] - 32
            (p, gt, gc), xt = _inproj_stream(
                1, xt, gm, mods, w[:, :main].astype(BF16),
                (_pad_lanes(w[:, main:]), _gate_consts(mlstm_b_i[j], jnp.zeros_like(mlstm_b_i[j])),
                 _pad_lanes(jnp.concatenate([jnp.zeros((1, 16), F32), mlstm_b_f[j].reshape(1, -1)], axis=1))), nct)
            of, ob = _mlstm_scan(p, d // 2, d, gt, gc, nct)
            mixer_out = (of, ob, p, 2 * d, mlstm_norm_g[j].reshape(1, -1))
            w_out, scale = mlstm_w_out[j], 1.0
        elif kind == 2:
            lambda_init = 0.8 - 0.6 * math.exp(-0.3 * i)
            (q, k, v), xt = _inproj_stream(2, xt, gm, mods, diff_w_in[j].astype(BF16), (cos, sin, (d, d, d)), nct)
            o = jnp.concatenate([_diff_attn(q[:, :lc], k[:, :lc], v[:, :lc], diff_lambda[j], lambda_init),
                                 _diff_attn(q[:, lc:], k, v, diff_lambda[j], lambda_init)], axis=1)
            mixer_out = (o, diff_norm_g[j].reshape(1, -1))
            w_out, scale = diff_w_out[j], 1.0 - lambda_init
        else:
            (q, k, v), xt = _inproj_stream(3, xt, gm, mods, swa_w_in[j].astype(BF16),
                                           (cos, sin, (d, d // 4, d // 4)), nct)
            mixer_out = _swa_attn(q, k, v, _pad_lanes(swa_sink[j].reshape(1, -1)), nct, seq)
            w_out, scale = swa_w_out[j], 1.0
        x1, f, rinfo, counts = _outproj(kind, mixer_out, xt, mods, w_out.astype(BF16), gf, rw, rb, nct, scale)
        xt = _moe(f, rinfo, counts, x1, mods, i, moe_w_gu, moe_b_gu[i], moe_w_down, moe_b_down[i], nct,
                  i == depth - 1)
    return _final_norm(xt, final_g.reshape(1, d), nct)
```

```python
import functools
import math

import jax
import jax.numpy as jnp
from jax import lax
from jax.experimental import pallas as pl
from jax.experimental.pallas import tpu as pltpu

F32 = jnp.float32
BF16 = jnp.bfloat16
HI = lax.Precision.HIGHEST

NORM_EPS = 1e-6
ROPE_BASE = 10000.0
GRID_W = 64
TM = 256
CHUNK = 64
CHUNK_SHIFT = 6
GDN_HEADS_PER_STEP = 4
DIFF_QK_AHEAD = 3
Q_SCALE = {2: 0.125 * math.log2(math.e), 3: 0.125}
LANES = 128
N_EXPERTS = 32
TOP_K = 4
SWIGLU_ALPHA = 1.702
SWIGLU_LIMIT = 7.0
MOE_BM = 256
SWA_WINDOW = 128
VMEM_LIMIT = 56 * 1024 * 1024


def _cp(sem, vmem=VMEM_LIMIT):
    return pltpu.CompilerParams(dimension_semantics=sem, vmem_limit_bytes=vmem)


def _dot(a, b):
    return jnp.dot(a.astype(BF16), b.astype(BF16), preferred_element_type=F32)


def _dot_nt(a, b):
    return lax.dot_general(a.astype(BF16), b.astype(BF16), (((1,), (1,)), ((), ())),
                           preferred_element_type=F32)


def _dot_tn(a, b):
    return lax.dot_general(a.astype(BF16), b.astype(BF16), (((0,), (0,)), ((), ())),
                           preferred_element_type=F32)


def _dot_hi(a, b):
    return jnp.dot(a, b, precision=HI, preferred_element_type=F32)


def _sigmoid(x):
    return 1.0 / (1.0 + jnp.exp(-x))


def _softplus(x):
    return jnp.maximum(x, 0.0) + jnp.log(1.0 + jnp.exp(-jnp.abs(x)))


def _mods_kernel(c_ref, w_ref, b_ref, o_ref):
    c = c_ref[...]
    o_ref[...] = _dot_hi(c * _sigmoid(c), w_ref[...]) + b_ref[...]


def _ada_mods(cond8, ada_w, ada_b):
    depth, d, d6 = ada_w.shape
    return pl.pallas_call(
        _mods_kernel,
        out_shape=jax.ShapeDtypeStruct((depth, 8, d6), F32),
        grid=(depth, d6 // d),
        in_specs=[pl.BlockSpec((8, d), lambda i, j: (0, 0)),
                  pl.BlockSpec((None, d, d), lambda i, j: (i, 0, j)),
                  pl.BlockSpec((None, 1, d), lambda i, j: (i, 0, j))],
        out_specs=pl.BlockSpec((None, 8, d), lambda i, j: (i, 0, j)),
        compiler_params=_cp(("arbitrary", "arbitrary")),
        name="ada_mods",
    )(cond8, ada_w, ada_b.reshape(depth, 1, d6))


def _mod_row(m_ref, row, k, d):
    return m_ref[pl.ds(row, 1), k * d:(k + 1) * d]


def _prenorm_mod(x, g, shift, scale):
    ms = jnp.mean(x * x, axis=-1, keepdims=True)
    y = x * lax.rsqrt(ms + NORM_EPS) * g
    return y * (1.0 + scale) + shift


def _rope_slab(x, cos, sin, first_half):
    part = jnp.where(first_half, pltpu.roll(x, LANES - 16, 1), pltpu.roll(x, 16, 1))
    return x * cos + part * sin


def _inproj_kernel(kind, nct, nb, d, *refs):
    b = pl.program_id(0)
    t = pl.program_id(1)
    row = jnp.where(t < nct, nb, b)
    x_ref, g_ref, m_ref, w_ref = refs[:4]
    rest = refs[4:]
    h = _prenorm_mod(x_ref[...], g_ref[...], _mod_row(m_ref, row, 0, d), _mod_row(m_ref, row, 1, d))
    p = _dot(h, w_ref[...])
    if kind in (0, 1):
        wg_ref, c0_ref, c1_ref, p_ref, gt_ref, gc_ref = rest
        p_ref[...] = p
        graw = _dot3(h, wg_ref[...])
        lane = lax.broadcasted_iota(jnp.int32, graw.shape, 1)
        if kind == 0:
            gate = jnp.where(lane < 16, -jnp.exp(c0_ref[...]) * _softplus(graw + c1_ref[...]),
                             _sigmoid(graw))
        else:
            gate = jnp.where(lane < 16, graw + c0_ref[...], -_softplus(-(graw + c1_ref[...])))
        ii = lax.broadcasted_iota(jnp.int32, (TM, TM), 0)
        jj = lax.broadcasted_iota(jnp.int32, (TM, TM), 1)
        same = jnp.right_shift(ii, CHUNK_SHIFT) == jnp.right_shift(jj, CHUNK_SHIFT)
        cum_f = _dot_mask(jnp.where(same, jnp.where(jj <= ii, 1.0, 0.0), 0.0), gate)
        cum_b = _dot_mask(jnp.where(same, jnp.where(jj >= ii, 1.0, 0.0), 0.0), gate)
        cum = jnp.where(lane % 16 < 8, cum_f, cum_b)
        decay_col = jnp.right_shift(lane, 4) == kind
        gate = jnp.where(decay_col, cum, gate)
        gt_ref[...] = gate
        gc_ref[...] = gate.T
    else:
        cos_ref, sin_ref, q_ref, k_ref, v_ref = rest
        cos = cos_ref[...]
        sin = sin_ref[...]
        lane = lax.broadcasted_iota(jnp.int32, cos.shape, 1)
        first_half = (lane % 32) < 16
        nq = q_ref.shape[-1] // LANES
        nk = k_ref.shape[-1] // LANES
        for s in range(nq):
            q_ref[:, s * LANES:(s + 1) * LANES] = (
                _rope_slab(p[:, s * LANES:(s + 1) * LANES], cos, sin, first_half) * Q_SCALE[kind]).astype(BF16)
        for s in range(nk):
            o = (nq + s) * LANES
            k_ref[:, s * LANES:(s + 1) * LANES] = _rope_slab(p[:, o:o + LANES], cos, sin, first_half).astype(BF16)
        o = (nq + nk) * LANES
        v_ref[...] = p[:, o:].astype(BF16)


def _inproj(kind, xt, g, mods, w_main, extra, nct):
    nb, t_all, d = xt.shape
    nt = t_all // TM
    pdim = w_main.shape[1]
    tile = lambda w: pl.BlockSpec((None, TM, w), lambda b, t: (b, t, 0))
    full = lambda a: pl.BlockSpec(a.shape, lambda b, t: (0,) * a.ndim)
    in_specs = [tile(d), full(g), full(mods), full(w_main)]
    if kind in (0, 1):
        wg, c0, c1 = extra
        in_specs += [full(wg), full(c0), full(c1)]
        out_shape = (jax.ShapeDtypeStruct((nb, t_all, pdim), F32),
                     jax.ShapeDtypeStruct((nb, t_all, LANES), F32),
                     jax.ShapeDtypeStruct((nb, LANES, t_all), F32))
        out_specs = (tile(pdim), tile(LANES), pl.BlockSpec((None, LANES, TM), lambda b, t: (b, 0, t)))
        args = (xt, g, mods, w_main, wg, c0, c1)
    else:
        cos, sin, widths = extra
        in_specs += [pl.BlockSpec((TM, LANES), lambda b, t: (t, 0))] * 2
        out_shape = tuple(jax.ShapeDtypeStruct((nb, t_all, w), BF16) for w in widths)
        out_specs = tuple(tile(w) for w in widths)
        args = (xt, g, mods, w_main, cos, sin)
    return pl.pallas_call(
        functools.partial(_inproj_kernel, kind, nct, nb, d),
        out_shape=out_shape, grid=(nb, nt), in_specs=in_specs, out_specs=out_specs,
        compiler_params=_cp(("arbitrary", "arbitrary")), name=f"inproj{kind}",
    )(*args)


def _gdn_prep_kernel(nct, nt, p_ref, prev_ref, next_ref, w_ref, q_ref, k_ref, v_ref, win_ref):
    t = pl.program_id(1)
    prev_ok = jnp.logical_and(t != 0, t != nct)
    next_ok = jnp.logical_and(t != nct - 1, t != nt - 1)
    win_ref[0:8, :] = jnp.where(prev_ok, prev_ref[...], 0.0)
    win_ref[8:8 + TM, :] = p_ref[...]
    win_ref[8 + TM:, :] = jnp.where(next_ok, next_ref[...], 0.0)
    width = q_ref.shape[-1]
    for part, o_ref in enumerate((q_ref, k_ref, v_ref)):
        for h in range(width // LANES):
            c0 = part * width + h * LANES
            acc = None
            for j in range(5):
                term = win_ref[6 + j:6 + j + TM, c0:c0 + LANES] * w_ref[j:j + 1, c0:c0 + LANES]
                acc = term if acc is None else acc + term
            y = acc * _sigmoid(acc)
            if part < 2:
                y = y * lax.rsqrt(jnp.sum(y * y, axis=-1, keepdims=True) + NORM_EPS)
            if part == 0:
                y = y * (LANES ** -0.5)
            o_ref[:, h * LANES:(h + 1) * LANES] = y


def _gdn_prep(p_main, conv_w8, nct):
    nb, t_all, _ = p_main.shape
    nt = t_all // TM
    width = conv_w8.shape[1] // 3
    cw = 3 * width
    t8 = t_all // 8
    r8 = TM // 8
    tile = pl.BlockSpec((None, TM, width), lambda b, t: (b, t, 0))
    return pl.pallas_call(
        functools.partial(_gdn_prep_kernel, nct, nt),
        out_shape=tuple(jax.ShapeDtypeStruct((nb, t_all, width), F32) for _ in range(3)),
        grid=(nb, nt),
        in_specs=[pl.BlockSpec((None, TM, cw), lambda b, t: (b, t, 0)),
                  pl.BlockSpec((None, 8, cw), lambda b, t: (b, jnp.maximum(t * r8 - 1, 0), 0)),
                  pl.BlockSpec((None, 8, cw), lambda b, t: (b, jnp.minimum((t + 1) * r8, t8 - 1), 0)),
                  pl.BlockSpec(conv_w8.shape, lambda b, t: (0, 0))],
        out_specs=(tile, tile, tile),
        scratch_shapes=[pltpu.VMEM((TM + 16, cw), F32)],
        compiler_params=_cp(("arbitrary", "arbitrary")), name="gdn_prep",
    )(p_main, p_main, p_main, conv_w8)


def _split_bf16(a):
    hi = a.astype(BF16)
    return hi, (a - hi.astype(F32)).astype(BF16)


def _dot_mask(mask01, x):
    m = mask01.astype(BF16)
    x1 = x.astype(BF16)
    r1 = x - x1.astype(F32)
    x2 = r1.astype(BF16)
    x3 = (r1 - x2.astype(F32)).astype(BF16)
    d = lambda y: jnp.dot(m, y, preferred_element_type=F32)
    return d(x1) + (d(x2) + d(x3))


def _dot3(a, b):
    ah, al = _split_bf16(a)
    bh, bl = _split_bf16(b)
    d = lambda x, y: jnp.dot(x, y, preferred_element_type=F32)
    return d(ah, bh) + (d(ah, bl) + d(al, bh))


def _unit_triangular_inverses(mats, same):
    a8 = [jnp.where(same(3), a, 0.0) for a in mats]
    ts = [jnp.where(same(0), 1.0, 0.0) - x for x in a8]
    p2 = [_dot3(x, x) for x in a8]
    ts = [t + _dot3(t, p) for t, p in zip(ts, p2)]
    p4 = [_dot3(p, p) for p in p2]
    ts = [t + _dot3(t, p) for t, p in zip(ts, p4)]
    for sh in (4, 5, 6):
        offs = [jnp.where(same(sh), jnp.where(same(sh - 1), 0.0, a), 0.0) for a in mats]
        mids = [_dot3(o, t) for o, t in zip(offs, ts)]
        ts = [t - _dot3(t, m) for t, m in zip(ts, mids)]
    return ts


def _gdn_tiles(probs):
    ii = lax.broadcasted_iota(jnp.int32, (TM, TM), 0)
    jj = lax.broadcasted_iota(jnp.int32, (TM, TM), 1)
    same = lambda sh: jnp.right_shift(ii, sh) == jnp.right_shift(jj, sh)
    incl = {rev: jnp.where(same(CHUNK_SHIFT), jnp.where((ii <= jj) if rev else (ii >= jj), 1.0, 0.0), 0.0) > 0.5
            for rev in {p[8] for p in probs}}
    gams = [jnp.where(incl[rev], jnp.exp(jnp.where(incl[rev], gc - gr, 0.0)), 0.0)
            for (_, _, _, _, gc, gr, _, _, rev) in probs]
    kbs = [p[1] * p[3] for p in probs]
    mats = [jnp.where(same(0), 0.0, _dot_nt(kb, p[1]) * gam) for kb, p, gam in zip(kbs, probs, gams)]
    tinvs = _unit_triangular_inverses(mats, same)
    egs = [jnp.exp(p[4]) for p in probs]
    uws = [_dot3(t, jnp.concatenate([p[2] * p[3], kb * eg], axis=1)) for t, p, kb, eg in zip(tinvs, probs, kbs, egs)]
    qks = [_dot_nt(p[0], p[1]) * gam for p, gam in zip(probs, gams)]
    qgs = [p[0] * eg for p, eg in zip(probs, egs)]
    states = [p[6] for p in probs]
    nchunk = TM // CHUNK
    for n in range(nchunk):
        cs = [nchunk - 1 - n if p[8] else n for p in probs]
        sls = [slice(c * CHUNK, (c + 1) * CHUNK) for c in cs]
        gtots = [p[4][(c * CHUNK if p[8] else (c + 1) * CHUNK - 1):][:1] for p, c in zip(probs, cs)]
        wss = [_dot(uw[sl, LANES:], s) for uw, sl, s in zip(uws, sls, states)]
        qss = [_dot(qg[sl], s) for qg, sl, s in zip(qgs, sls, states)]
        vns = [uw[sl, :LANES] - ws for uw, sl, ws in zip(uws, sls, wss)]
        for p, sl, qs, qk, vn in zip(probs, sls, qss, qks, vns):
            p[7](sl, qs + _dot(qk[sl, sl], vn))
        states = [s * jnp.exp(gt) + _dot_tn(p[1][sl] * jnp.exp(gt - p[4][sl]), vn)
                  for s, gt, p, sl, vn in zip(states, gtots, probs, sls, vns)]
    return states


def _gate_col(gt, idx):
    lane = lax.broadcasted_iota(jnp.int32, gt.shape, 1)
    return jnp.sum(jnp.where(lane == idx, gt, 0.0), axis=1, keepdims=True)


def _gate_row(gc, idx):
    sub = lax.broadcasted_iota(jnp.int32, gc.shape, 0)
    return jnp.sum(jnp.where(sub == idx, gc, 0.0), axis=0, keepdims=True)


def _gdn_scan_kernel(qf, kf, vf, gtf, gcf, qb, kb, vb, gtb, gcb, of_ref, ob_ref, s_ref):
    hp = pl.program_id(1)

    @pl.when(pl.program_id(2) == 0)
    def _():
        s_ref[...] = jnp.zeros_like(s_ref)

    probs = []
    for rev, (q_r, k_r, v_r, gt_r, gc_r, o_r) in enumerate(
            ((qf, kf, vf, gtf, gcf, of_ref), (qb, kb, vb, gtb, gcb, ob_ref))):
        gt = gt_r[...]
        gc = gc_r[...]
        for j in range(GDN_HEADS_PER_STEP):
            head = hp * GDN_HEADS_PER_STEP + j
            cols = slice(j * LANES, (j + 1) * LANES)

            def store(sl, val, o_r=o_r, cols=cols):
                o_r[sl, cols] = val

            probs.append((q_r[:, cols], k_r[:, cols], v_r[:, cols], _gate_col(gt, 16 + rev * 8 + head),
                          _gate_col(gt, rev * 8 + head), _gate_row(gc, rev * 8 + head), s_ref[rev, j], store,
                          bool(rev)))
    states = _gdn_tiles(probs)
    for rev in range(2):
        for j in range(GDN_HEADS_PER_STEP):
            s_ref[rev, j] = states[rev * GDN_HEADS_PER_STEP + j]


def _scan_tiles(nct, nt):
    fwd = lambda s: s
    bwd = lambda s: jnp.where(s < nct, nct - 1 - s, nt - 1 - (s - nct))
    return fwd, bwd


def _gdn_scan(q, k, v, gt, gc, nct):
    nb, t_all, width = q.shape
    hw = GDN_HEADS_PER_STEP * LANES
    nt = t_all // TM
    fwd, bwd = _scan_tiles(nct, nt)

    def specs(tile_of):
        head = pl.BlockSpec((None, TM, hw), lambda b, h, s: (b, tile_of(s), h))
        return [head, head, head,
                pl.BlockSpec((None, TM, LANES), lambda b, h, s: (b, tile_of(s), 0)),
                pl.BlockSpec((None, LANES, TM), lambda b, h, s: (b, 0, tile_of(s)))]

    out = jax.ShapeDtypeStruct((nb, t_all, width), F32)
    return pl.pallas_call(
        _gdn_scan_kernel, out_shape=(out, out), grid=(nb, width // hw, nt),
        in_specs=specs(fwd) + specs(bwd),
        out_specs=(pl.BlockSpec((None, TM, hw), lambda b, h, s: (b, fwd(s), h)),
                   pl.BlockSpec((None, TM, hw), lambda b, h, s: (b, bwd(s), h))),
        scratch_shapes=[pltpu.VMEM((2, GDN_HEADS_PER_STEP, LANES, LANES), F32)],
        compiler_params=_cp(("arbitrary", "arbitrary", "arbitrary")), name="gdn_scan",
    )(q, k, v, gt, gc, q, k, v, gt, gc)


def _mlstm_tiles(probs):
    ii = lax.broadcasted_iota(jnp.int32, (TM, TM), 0)
    jj = lax.broadcasted_iota(jnp.int32, (TM, TM), 1)
    same_chunk = jnp.right_shift(ii, CHUNK_SHIFT) == jnp.right_shift(jj, CHUNK_SHIFT)
    incl = {rev: jnp.where(same_chunk, jnp.where((ii <= jj) if rev else (ii >= jj), 1.0, 0.0), 0.0) > 0.5
            for rev in {p[10] for p in probs}}
    nchunk = TM // CHUNK
    order = lambda p: [nchunk - 1 - n if p[10] else n for n in range(nchunk)]
    rows = lambda c: slice(c * CHUNK, (c + 1) * CHUNK)
    dlogs = [jnp.where(incl[p[10]], p[5] - p[6] + p[4], -jnp.inf) for p in probs]
    m_intras = [jnp.max(d, axis=-1, keepdims=True) for d in dlogs]
    qks = [_dot_nt(p[0], p[1]) for p in probs]
    btots, m_locs, c_locs = [], [], []
    for p in probs:
        bt, ml, cl = {}, {}, {}
        for c in order(p):
            last = c * CHUNK if p[10] else (c + 1) * CHUNK - 1
            bt[c] = p[5][last:last + 1]
            a_loc = bt[c] - p[5][rows(c)] + p[3][rows(c)]
            ml[c] = jnp.max(a_loc, axis=0, keepdims=True)
            cl[c] = _dot_tn(p[1][rows(c)] * jnp.exp(a_loc - ml[c]), p[2][rows(c)])
        btots.append(bt)
        m_locs.append(ml)
        c_locs.append(cl)
    new_states, c_ins, m_ins = [], [], []
    for p, bt, ml, cl in zip(probs, btots, m_locs, c_locs):
        c_st, m_st = p[7], p[8]
        ci, mi = {}, {}
        for c in order(p):
            ci[c], mi[c] = c_st, m_st
            m_new = jnp.maximum(bt[c] + m_st, ml[c])
            c_st = c_st * jnp.exp(bt[c] + m_st - m_new) + cl[c] * jnp.exp(ml[c] - m_new)
            m_st = m_new
        new_states.append((c_st, m_st))
        c_ins.append(ci)
        m_ins.append(mi)
    inters = [jnp.concatenate([_dot(p[0][rows(c)], ci[c]) for c in range(nchunk)], axis=0)
              for p, ci in zip(probs, c_ins)]
    for p, dlog, m_intra, qk, inter, mi in zip(probs, dlogs, m_intras, qks, inters, m_ins):
        m_in = jnp.concatenate([jnp.broadcast_to(mi[c], (CHUNK, 1)) for c in range(nchunk)], axis=0)
        log_inter = p[5] + m_in
        m_t = jnp.maximum(log_inter, m_intra)
        num = jnp.exp(log_inter - m_t) * inter + _dot(jnp.exp(dlog - m_t) * qk, p[2])
        dv = p[2].shape[1] // 2
        p[9](num[:, :dv] / jnp.maximum(jnp.abs(num[:, dv:dv + 1]), jnp.exp(-m_t)))
    return new_states


def _mlstm_scan_kernel(qf, kf, vf, gtf, gcf, qb, kb, vb, gtb, gcb, of_ref, ob_ref, c_ref, m_ref):
    hp = pl.program_id(1)

    @pl.when(pl.program_id(2) == 0)
    def _():
        c_ref[...] = jnp.zeros_like(c_ref)
        m_ref[...] = jnp.zeros_like(m_ref)

    dqk = qf.shape[-1] // 2
    dv = vf.shape[-1] // 2
    lane = lax.broadcasted_iota(jnp.int32, (TM, dv), 1)
    ones_pad = jnp.where(lane == 0, 1.0, 0.0)
    probs = []
    for rev, (q_r, k_r, v_r, gt_r, gc_r, o_r) in enumerate(
            ((qf, kf, vf, gtf, gcf, of_ref), (qb, kb, vb, gtb, gcb, ob_ref))):
        gt = gt_r[...]
        gc = gc_r[...]
        for j in range(2):
            head = 2 * hp + j

            def store(val, o_r=o_r, j=j):
                o_r[:, j * dv:(j + 1) * dv] = val

            probs.append((q_r[:, j * dqk:(j + 1) * dqk], k_r[:, j * dqk:(j + 1) * dqk] * (dqk ** -0.5),
                          jnp.concatenate([v_r[:, j * dv:(j + 1) * dv], ones_pad], axis=1),
                          _gate_col(gt, rev * 8 + head), _gate_row(gc, rev * 8 + head),
                          _gate_col(gt, 16 + rev * 8 + head), _gate_row(gc, 16 + rev * 8 + head),
                          c_ref[rev, j], m_ref[rev, j][0:1, 0:1], store, bool(rev)))
    states = _mlstm_tiles(probs)
    for rev in range(2):
        for j in range(2):
            c_st, m_st = states[rev * 2 + j]
            c_ref[rev, j] = c_st
            m_ref[rev, j] = jnp.broadcast_to(m_st, (8, LANES))


def _mlstm_scan(p, qw, vw, gt, gc, nct):
    nb, t_all, _ = p.shape
    nhp = qw // LANES
    nt = t_all // TM
    dqk = LANES // 2
    dv = vw // nhp // 2
    fwd, bwd = _scan_tiles(nct, nt)

    def specs(tile_of):
        return [pl.BlockSpec((None, TM, LANES), lambda b, h, s: (b, tile_of(s), h)),
                pl.BlockSpec((None, TM, LANES), lambda b, h, s: (b, tile_of(s), nhp + h)),
                pl.BlockSpec((None, TM, 2 * dv), lambda b, h, s: (b, tile_of(s), 2 * qw // (2 * dv) + h)),
                pl.BlockSpec((None, TM, LANES), lambda b, h, s: (b, tile_of(s), 0)),
                pl.BlockSpec((None, LANES, TM), lambda b, h, s: (b, 0, tile_of(s)))]

    out = jax.ShapeDtypeStruct((nb, t_all, vw), F32)
    return pl.pallas_call(
        _mlstm_scan_kernel, out_shape=(out, out), grid=(nb, nhp, nt),
        in_specs=specs(fwd) + specs(bwd),
        out_specs=(pl.BlockSpec((None, TM, 2 * dv), lambda b, h, s: (b, fwd(s), h)),
                   pl.BlockSpec((None, TM, 2 * dv), lambda b, h, s: (b, bwd(s), h))),
        scratch_shapes=[pltpu.VMEM((2, 2, dqk, 2 * dv), F32), pltpu.VMEM((2, 2, 8, LANES), F32)],
        compiler_params=_cp(("arbitrary", "arbitrary", "arbitrary")), name="mlstm_scan",
    )(p, p, p, gt, gc, p, p, p, gt, gc)


def _diff_attn_kernel(nk, lambda_init, lam_ref, q_ref, k_ref, v_ref, o_ref, m_ref, acc_ref):
    j = pl.program_id(3)
    dh = LANES // 2

    @pl.when(j == 0)
    def _():
        m_ref[...] = jnp.full(m_ref.shape, -jnp.inf, F32)
        acc_ref[...] = jnp.zeros_like(acc_ref)

    tq = q_ref.shape[0]
    tk = k_ref.shape[0]
    ones_col = jnp.where(lax.broadcasted_iota(jnp.int32, (tk, LANES), 1) == 0, 1.0, 0.0).astype(BF16)
    v_aug = jnp.concatenate([v_ref[...], ones_col], axis=1)
    rs = 512 if tq % 512 == 0 else tq
    probs = [(hh, slice(r * rs, (r + 1) * rs)) for r in range(tq // rs) for hh in range(2)]

    def score(prob):
        hh, rows = prob
        return lax.dot_general(q_ref[rows, hh * dh:(hh + 1) * dh], k_ref[:, hh * dh:(hh + 1) * dh],
                               (((1,), (1,)), ((), ())), preferred_element_type=F32)

    scores = [score(p) for p in probs[:DIFF_QK_AHEAD]]
    blocks = [slice(c * LANES, (c + 1) * LANES) for c in range(tk // LANES)]
    for n, (hh, rows) in enumerate(probs):
        s = scores[n]
        if n + DIFF_QK_AHEAD < len(probs):
            scores.append(score(probs[n + DIFF_QK_AHEAD]))
        part = s[:, blocks[0]]
        for blk in blocks[1:]:
            part = jnp.maximum(part, s[:, blk])
        m_old = m_ref[hh, rows, :]
        m_new = jnp.maximum(m_old, jnp.broadcast_to(jnp.max(part, axis=-1, keepdims=True), (rs, LANES)))
        alpha = jnp.exp2(m_old - m_new)
        p = jnp.concatenate([jnp.exp2(s[:, blk] - m_new).astype(BF16) for blk in blocks], axis=1)
        acc_ref[hh, rows, :] = (jnp.concatenate([alpha, alpha], axis=1) * acc_ref[hh, rows, :]
                                + jnp.dot(p, v_aug, preferred_element_type=F32))
        m_ref[hh, rows, :] = m_new

    @pl.when(j == nk - 1)
    def _():
        lp = lam_ref[...]
        lam = (jnp.exp(jnp.sum(lp[0:1] * lp[1:2], axis=-1, keepdims=True))
               - jnp.exp(jnp.sum(lp[2:3] * lp[3:4], axis=-1, keepdims=True)) + lambda_init)
        a0 = acc_ref[0]
        a1 = acc_ref[1]
        o_ref[...] = a0[:, :LANES] / a0[:, LANES:LANES + 1] - lam * (a1[:, :LANES] / a1[:, LANES:LANES + 1])


def _pick_tile(n, candidates):
    return next(c for c in candidates if n % c == 0)


def _diff_attn(q, k, v, lam_p, lambda_init):
    nb, nq_rows, width = q.shape
    ns = k.shape[1]
    tq = _pick_tile(nq_rows, (4096, 2048, 1024, 512, 256))
    tk = _pick_tile(ns, (1280, 1024, 768, 512, 256))
    nk = ns // tk
    kv = pl.BlockSpec((None, tk, LANES), lambda b, h, i, j: (b, j, h))
    qo = pl.BlockSpec((None, tq, LANES), lambda b, h, i, j: (b, i, h))
    return pl.pallas_call(
        functools.partial(_diff_attn_kernel, nk, lambda_init),
        out_shape=jax.ShapeDtypeStruct((nb, nq_rows, width), F32),
        grid=(nb, width // LANES, nq_rows // tq, nk),
        in_specs=[pl.BlockSpec(lam_p.shape, lambda b, h, i, j: (0, 0)), qo, kv, kv],
        out_specs=qo,
        scratch_shapes=[pltpu.VMEM((2, tq, LANES), F32), pltpu.VMEM((2, tq, 2 * LANES), F32)],
        compiler_params=_cp(("arbitrary", "arbitrary", "arbitrary", "arbitrary")), name="diff_attn",
    )(lam_p, q, k, v)


def _swa_kernel(nct, nt, seq, sink_ref, q_ref, kc_ref, kp_ref, k0_ref, kn_ref, vc_ref, vp_ref, v0_ref, vn_ref,
                o_ref):
    i = pl.program_id(1)
    dh = LANES // 2
    nkv = kc_ref.shape[-1] // dh
    group = (q_ref.shape[-1] // dh) // nkv
    lc = kc_ref.shape[0]
    col = lax.broadcasted_iota(jnp.int32, (TM, lc + 3 * TM), 1)
    qpos = (i - nct) * TM + lax.broadcasted_iota(jnp.int32, (TM, lc + 3 * TM), 0)
    kpos = (i - nct - 1) * TM + col - lc
    ninf = -jnp.inf
    in_window = jnp.where(jnp.abs(qpos - kpos) <= SWA_WINDOW,
                          jnp.where(kpos >= 0, jnp.where(kpos < seq, 0.0, ninf), ninf), ninf)
    bias = jnp.where(col < lc, 0.0, jnp.where(i >= nct, in_window, ninf))
    kcat = jnp.concatenate([kc_ref[...], kp_ref[...], k0_ref[...], kn_ref[...]], axis=0)
    vcat = jnp.concatenate([vc_ref[...], vp_ref[...], v0_ref[...], vn_ref[...]], axis=0)
    sink = sink_ref[...]
    scores = [lax.dot_general(q_ref[:, hq * dh:(hq + 1) * dh], kcat[:, (hq // group) * dh:(hq // group + 1) * dh],
                              (((1,), (1,)), ((), ())), preferred_element_type=F32) for hq in range(nkv * group)]
    nkeys = lc + 3 * TM
    ones_col = jnp.where(lax.broadcasted_iota(jnp.int32, (nkeys, dh), 1) == 0, 1.0, 0.0).astype(BF16)
    v_aug = [jnp.concatenate([vcat[:, g * dh:(g + 1) * dh], ones_col], axis=1) for g in range(nkv)]
    blocks = [slice(c * LANES, (c + 1) * LANES) for c in range(nkeys // LANES)]
    biases = [bias[:, blk] for blk in blocks]
    for hq, s in enumerate(scores):
        sblk = [s[:, blk] + b for blk, b in zip(blocks, biases)]
        part = sblk[0]
        for x in sblk[1:]:
            part = jnp.maximum(part, x)
        sb = jnp.broadcast_to(sink[0:1, hq:hq + 1], (TM, LANES))
        m = jnp.maximum(jnp.broadcast_to(jnp.max(part, axis=-1, keepdims=True), (TM, LANES)), sb)
        e = jnp.concatenate([jnp.exp(x - m).astype(BF16) for x in sblk], axis=1)
        ov = jnp.dot(e, v_aug[hq // group], preferred_element_type=F32)
        denom = ov[:, dh:dh + 1] + jnp.exp(sb - m)[:, 0:1]
        o_ref[:, hq * dh:(hq + 1) * dh] = ov[:, :dh] / denom


def _swa_attn(q, k, v, sink_row, nct, seq):
    nb, t_all, qw = q.shape
    kw = k.shape[-1]
    nt = t_all // TM
    lc = nct * TM
    ctxs = pl.BlockSpec((None, lc, kw), lambda b, i: (b, 0, 0))
    prev = pl.BlockSpec((None, TM, kw), lambda b, i: (b, jnp.maximum(i - 1, 0), 0))
    cur = pl.BlockSpec((None, TM, kw), lambda b, i: (b, i, 0))
    nxt = pl.BlockSpec((None, TM, kw), lambda b, i: (b, jnp.minimum(i + 1, nt - 1), 0))
    qo = pl.BlockSpec((None, TM, qw), lambda b, i: (b, i, 0))
    return pl.pallas_call(
        functools.partial(_swa_kernel, nct, nt, seq),
        out_shape=jax.ShapeDtypeStruct((nb, t_all, qw), F32),
        grid=(nb, nt),
        in_specs=[pl.BlockSpec(sink_row.shape, lambda b, i: (0, 0)), qo, ctxs, prev, cur, nxt, ctxs, prev, cur, nxt],
        out_specs=qo,
        compiler_params=_cp(("arbitrary", "arbitrary")), name="swa_attn",
    )(sink_row, q, k, k, k, k, v, v, v, v)


def _headnorm(o, g_row):
    outs = []
    for h in range(o.shape[-1] // LANES):
        seg = o[:, h * LANES:(h + 1) * LANES]
        ms = jnp.mean(seg * seg, axis=-1, keepdims=True)
        outs.append(seg * lax.rsqrt(ms + NORM_EPS) * g_row)
    return jnp.concatenate(outs, axis=1)


def _outproj_kernel(kind, nct, nb, d, out_scale, *refs):
    b = pl.program_id(0)
    t = pl.program_id(1)
    row = jnp.where(t < nct, nb, b)
    if kind in (0, 1):
        of_ref, ob_ref, z_ref, hg_ref = refs[:4]
        rest = refs[4:]
        z = z_ref[...]
        o = _headnorm(of_ref[...] + ob_ref[...], hg_ref[...]) * (z * _sigmoid(z) if kind == 0 else _sigmoid(z))
    elif kind == 2:
        o_ref, hg_ref = refs[:2]
        rest = refs[2:]
        o = _headnorm(o_ref[...], hg_ref[...]) * out_scale
    else:
        o_ref = refs[0]
        rest = refs[1:]
        o = o_ref[...]
    x_ref, m_ref, w_ref, gf_ref, rw_ref, rb_ref, x1_ref, f_ref, ri_ref, cnt_ref, carry_ref = rest

    @pl.when(jnp.logical_and(b == 0, t == 0))
    def _():
        carry_ref[...] = jnp.zeros_like(carry_ref)

    x1 = x_ref[...] + _mod_row(m_ref, row, 2, d) * _dot(o, w_ref[...])
    x1_ref[...] = x1
    f = _prenorm_mod(x1, gf_ref[...], _mod_row(m_ref, row, 3, d), _mod_row(m_ref, row, 4, d))
    f_ref[...] = f
    logits = _dot3(f, rw_ref[...]) + rb_ref[...]
    lane = lax.broadcasted_iota(jnp.int32, logits.shape, 1)
    vals, idxs, sels = [], [], []
    cur = logits
    for _ in range(TOP_K):
        m = jnp.max(cur, axis=-1, keepdims=True)
        idx = jnp.min(jnp.where(cur == m, lane, LANES), axis=-1, keepdims=True)
        sel = lane == idx
        cur = jnp.where(sel, -jnp.inf, cur)
        vals.append(m)
        idxs.append(idx)
        sels.append(sel)
    es = [jnp.exp(v - vals[0]) for v in vals]
    esum = es[0] + es[1] + es[2] + es[3]
    assign = jnp.where(sels[0] | sels[1] | sels[2] | sels[3], 1.0, 0.0)
    ii = lax.broadcasted_iota(jnp.int32, (TM, TM), 0)
    jj = lax.broadcasted_iota(jnp.int32, (TM, TM), 1)
    before = _dot(jnp.where(ii > jj, 1.0, 0.0), assign) + carry_ref[...]
    info = jnp.zeros(logits.shape, F32)
    for k in range(TOP_K):
        rank = jnp.sum(jnp.where(sels[k], before, 0.0), axis=-1, keepdims=True)
        info = jnp.where(lane == k, idxs[k].astype(F32), info)
        info = jnp.where(lane == TOP_K + k, rank, info)
        info = jnp.where(lane == 2 * TOP_K + k, es[k] / esum, info)
    ri_ref[...] = info
    carry_ref[...] = carry_ref[...] + jnp.sum(assign, axis=0, keepdims=True)
    cnt_ref[...] = carry_ref[...]


def _outproj(kind, mixer_out, xt, mods, w_out, gf, rw, rb, nct, out_scale=1.0):
    nb, t_all, d = xt.shape
    nt = t_all // TM
    tile = lambda w: pl.BlockSpec((None, TM, w), lambda b, t: (b, t, 0))
    full = lambda a: pl.BlockSpec(a.shape, lambda b, t: (0,) * a.ndim)
    if kind in (0, 1):
        of, ob, pm, zoff, hg = mixer_out
        zblk = zoff // d
        lead = [of, ob, pm, hg]
        lead_specs = [tile(d), tile(d), pl.BlockSpec((None, TM, d), lambda b, t: (b, t, zblk)), full(hg)]
    elif kind == 2:
        o, hg = mixer_out
        lead = [o, hg]
        lead_specs = [tile(d), full(hg)]
    else:
        lead = [mixer_out]
        lead_specs = [tile(d)]
    args = lead + [xt, mods, w_out, gf, rw, rb]
    in_specs = lead_specs + [tile(d), full(mods), full(w_out), full(gf), full(rw), full(rb)]
    return pl.pallas_call(
        functools.partial(_outproj_kernel, kind, nct, nb, d, out_scale),
        out_shape=(jax.ShapeDtypeStruct((nb, t_all, d), F32), jax.ShapeDtypeStruct((nb, t_all, d), F32),
                   jax.ShapeDtypeStruct((nb, t_all, LANES), F32), jax.ShapeDtypeStruct((1, LANES), F32)),
        grid=(nb, nt), in_specs=in_specs,
        out_specs=(tile(d), tile(d), tile(LANES), pl.BlockSpec((1, LANES), lambda b, t: (0, 0))),
        scratch_shapes=[pltpu.VMEM((1, LANES), F32)],
        compiler_params=_cp(("arbitrary", "arbitrary")), name=f"outproj{kind}",
    )(*args)


def _dispatch_kernel(zstart_ref, nused_ref, pos_ref, f_ref, xs_out, zeros, sem, zsem):
    @pl.when(pl.program_id(0) == 0)
    def _():
        zeros[...] = jnp.zeros_like(zeros)

        def zero_copy(row0):
            return pltpu.make_async_copy(zeros, xs_out.at[pl.ds(pl.multiple_of(row0, MOE_BM), MOE_BM)], zsem)

        n_blk = xs_out.shape[0] // MOE_BM
        for e in range(zstart_ref.shape[0]):
            pl.when(zstart_ref[e] >= 0)(lambda e=e: zero_copy(zstart_ref[e]).start())
        lax.fori_loop(nused_ref[0], n_blk, lambda b, c: (zero_copy(b * MOE_BM).start(), c)[1], 0)
        for e in range(zstart_ref.shape[0]):
            pl.when(zstart_ref[e] >= 0)(lambda e=e: zero_copy(zstart_ref[e]).wait())
        lax.fori_loop(nused_ref[0], n_blk, lambda b, c: (zero_copy(b * MOE_BM).wait(), c)[1], 0)

    def row_copy(r, k):
        return pltpu.make_async_copy(f_ref.at[pl.ds(r, 1)], xs_out.at[pl.ds(pos_ref[TOP_K * r + k], 1)], sem)

    def start(r, c):
        for k in range(TOP_K):
            row_copy(r, k).start(priority=k % 2)
        return c

    def wait(r, c):
        for k in range(TOP_K):
            row_copy(r, k).wait()
        return c

    lax.fori_loop(0, TM, start, 0)
    lax.fori_loop(0, TM, wait, 0)


def _dispatch(f2, pos_flat, zstart, n_used, n_rows):
    n, d = f2.shape
    grid_spec = pltpu.PrefetchScalarGridSpec(
        num_scalar_prefetch=2, grid=(n // TM,),
        in_specs=[pl.BlockSpec((TOP_K * TM,), lambda i, z, u: (i,), memory_space=pltpu.SMEM),
                  pl.BlockSpec((TM, d), lambda i, z, u: (i, 0))],
        out_specs=pl.BlockSpec(memory_space=pl.ANY),
        scratch_shapes=[pltpu.VMEM((MOE_BM, d), F32), pltpu.SemaphoreType.DMA, pltpu.SemaphoreType.DMA])
    return pl.pallas_call(
        _dispatch_kernel, out_shape=jax.ShapeDtypeStruct((n_rows, d), F32), grid_spec=grid_spec,
        compiler_params=_cp(("arbitrary",)), name="moe_dispatch",
    )(zstart, n_used, pos_flat, f2)


def _expert_kernel(be_ref, nused_ref, xs_ref, wgu_ref, bgu_ref, wd_ref, bd_ref, ys_ref, wgu_bf, wd_bf):
    i = pl.program_id(0)
    ff = wd_ref.shape[0]

    @pl.when(i >= nused_ref[0])
    def _():
        ys_ref[...] = jnp.zeros_like(ys_ref)

    @pl.when(i < nused_ref[0])
    def _():
        @pl.when(jnp.logical_or(i == 0, be_ref[i] != be_ref[jnp.maximum(i - 1, 0)]))
        def _():
            wgu_bf[...] = wgu_ref[...].astype(BF16)
            wd_bf[...] = wd_ref[...].astype(BF16)

        gu = _dot(xs_ref[...], wgu_bf[...]) + bgu_ref[...]
        x_glu = jnp.minimum(gu[:, :ff], SWIGLU_LIMIT)
        x_lin = jnp.clip(gu[:, ff:], -SWIGLU_LIMIT, SWIGLU_LIMIT)
        act = x_glu * _sigmoid(SWIGLU_ALPHA * x_glu) * (x_lin + 1.0)
        ys_ref[...] = _dot(act, wd_bf[...]) + bd_ref[...]


def _expert_ffn(xs, blk_exp, n_used, layer, w_gu, b_gu, w_down, b_down):
    n_rows, d = xs.shape
    _, ne, _, f2 = w_gu.shape
    ff = w_down.shape[2]
    grid_spec = pltpu.PrefetchScalarGridSpec(
        num_scalar_prefetch=2, grid=(n_rows // MOE_BM,),
        in_specs=[pl.BlockSpec((MOE_BM, d), lambda i, be, nu: (i, 0)),
                  pl.BlockSpec((None, None, d, f2), lambda i, be, nu: (layer, be[i], 0, 0)),
                  pl.BlockSpec((None, 1, f2), lambda i, be, nu: (be[i], 0, 0)),
                  pl.BlockSpec((None, None, ff, d), lambda i, be, nu: (layer, be[i], 0, 0)),
                  pl.BlockSpec((None, 1, d), lambda i, be, nu: (be[i], 0, 0))],
        out_specs=pl.BlockSpec((MOE_BM, d), lambda i, be, nu: (i, 0)),
        scratch_shapes=[pltpu.VMEM((d, f2), BF16), pltpu.VMEM((ff, d), BF16)])
    return pl.pallas_call(
        _expert_kernel, out_shape=jax.ShapeDtypeStruct((n_rows, d), F32), grid_spec=grid_spec,
        compiler_params=_cp(("arbitrary",)), name="moe_experts",
    )(blk_exp, n_used, xs, w_gu, b_gu.reshape(ne, 1, f2), w_down, b_down.reshape(ne, 1, d))


def _combine_kernel(nct, nb, nt, d, pos_ref, ri_ref, x1_ref, m_ref, ys_hbm, x2_ref, buf, sem):
    i = pl.program_id(0)
    b = i // nt
    t = i % nt
    row = jnp.where(t < nct, nb, b)

    def row_copy(r, k):
        return pltpu.make_async_copy(ys_hbm.at[pl.ds(pos_ref[TOP_K * r + k], 1)], buf.at[k, pl.ds(r, 1)], sem)

    def start(r, c):
        for k in range(TOP_K):
            row_copy(r, k).start(priority=k % 2)
        return c

    def wait(r, c):
        for k in range(TOP_K):
            row_copy(r, k).wait()
        return c

    lax.fori_loop(0, TM, start, 0)
    lax.fori_loop(0, TM, wait, 0)
    ri = ri_ref[...]
    y = None
    for k in range(TOP_K):
        term = buf[k] * ri[:, 2 * TOP_K + k:2 * TOP_K + k + 1]
        y = term if y is None else y + term
    x2_ref[...] = x1_ref[...] + _mod_row(m_ref, row, 5, d) * y


def _combine(ys, pos_flat, rinfo2, x1_2, mods, nct, nb):
    n, d = x1_2.shape
    nt = n // TM // nb
    return pl.pallas_call(
        functools.partial(_combine_kernel, nct, nb, nt, d),
        out_shape=jax.ShapeDtypeStruct((n, d), F32),
        grid=(n // TM,),
        in_specs=[pl.BlockSpec((TOP_K * TM,), lambda i: (i,), memory_space=pltpu.SMEM),
                  pl.BlockSpec((TM, LANES), lambda i: (i, 0)),
                  pl.BlockSpec((TM, d), lambda i: (i, 0)),
                  pl.BlockSpec(mods.shape, lambda i: (0, 0)),
                  pl.BlockSpec(memory_space=pl.ANY)],
        out_specs=pl.BlockSpec((TM, d), lambda i: (i, 0)),
        scratch_shapes=[pltpu.VMEM((TOP_K, TM, d), F32), pltpu.SemaphoreType.DMA],
        compiler_params=_cp(("arbitrary",)), name="moe_combine",
    )(pos_flat, rinfo2, x1_2, mods, ys)


def _moe(f, rinfo, counts, x1, mods, layer, w_gu, b_gu, w_down, b_down, nct):
    nb, t_all, d = f.shape
    n = nb * t_all
    ne = w_gu.shape[1]
    ri2 = rinfo.reshape(n, LANES)
    idx = ri2[:, :TOP_K].astype(jnp.int32)
    rank = ri2[:, TOP_K:2 * TOP_K].astype(jnp.int32)
    cnt = counts[0, :ne].astype(jnp.int32)
    padded = (cnt + MOE_BM - 1) // MOE_BM * MOE_BM
    pad_end = jnp.cumsum(padded)
    pad_start = pad_end - padded
    onehot = idx[:, :, None] == jnp.arange(ne, dtype=jnp.int32)[None, None, :]
    pos = (rank + jnp.sum(jnp.where(onehot, pad_start[None, None, :], 0), axis=-1)).reshape(-1)
    n_blk = (n * TOP_K + MOE_BM - 1) // MOE_BM + ne
    n_used = (pad_end[-1:] // MOE_BM).astype(jnp.int32)
    blk_row = jnp.minimum(jnp.arange(n_blk, dtype=jnp.int32), n_used - 1)
    blk_exp = jnp.sum(blk_row[:, None] * MOE_BM >= pad_end[None, :], axis=-1).astype(jnp.int32)
    zstart = jnp.where(padded > 0, pad_end - MOE_BM, -1).astype(jnp.int32)
    xs = _dispatch(f.reshape(n, d), pos, zstart, n_used, n_blk * MOE_BM)
    ys = _expert_ffn(xs, blk_exp, n_used, layer, w_gu, b_gu, w_down, b_down)
    x2 = _combine(ys, pos, ri2, x1.reshape(n, d), mods, nct, nb)
    return x2.reshape(nb, t_all, d)


def _final_kernel(x_ref, g_ref, o_ref):
    x = x_ref[...]
    o_ref[...] = x * lax.rsqrt(jnp.mean(x * x, axis=-1, keepdims=True) + NORM_EPS) * g_ref[...]


def _final_norm(xt, g, nct):
    nb, t_all, d = xt.shape
    seq = t_all - nct * TM
    return pl.pallas_call(
        _final_kernel, out_shape=jax.ShapeDtypeStruct((nb, seq, d), F32), grid=(nb, seq // TM),
        in_specs=[pl.BlockSpec((None, TM, d), lambda b, t: (b, t + nct, 0)), pl.BlockSpec(g.shape, lambda b, t: (0, 0))],
        out_specs=pl.BlockSpec((None, TM, d), lambda b, t: (b, t, 0)),
        compiler_params=_cp(("arbitrary", "arbitrary")), name="final_norm",
    )(xt, g)


def _rope_tables(seq, lc):
    rows = seq // GRID_W
    row = jnp.repeat(jnp.arange(rows), GRID_W).astype(F32)
    col = jnp.tile(jnp.arange(GRID_W), rows).astype(F32)
    inv = ROPE_BASE ** (-jnp.arange(16, dtype=F32) / 16)
    ang_r = row[:, None] * inv[None, :]
    ang_c = col[:, None] * inv[None, :]
    cos = jnp.concatenate([jnp.cos(ang_r)] * 2 + [jnp.cos(ang_c)] * 2, axis=1)
    sin = jnp.concatenate([-jnp.sin(ang_r), jnp.sin(ang_r), -jnp.sin(ang_c), jnp.sin(ang_c)], axis=1)
    cos = jnp.concatenate([jnp.ones((lc, 64), F32), cos], axis=0)
    sin = jnp.concatenate([jnp.zeros((lc, 64), F32), sin], axis=0)
    return jnp.tile(cos, (1, 2)), jnp.tile(sin, (1, 2))


def _pad_lanes(a, fill=0.0):
    return jnp.pad(a, [(0, 0)] * (a.ndim - 1) + [(0, LANES - a.shape[-1])], constant_values=fill)


def _gate_consts(first, second):
    return _pad_lanes(jnp.concatenate([first.reshape(1, -1), second.reshape(1, -1)], axis=1))


def kernel(x, c, ctx, c_ctx, ada_w, ada_b, norm_mix_g, norm_ffn_g, final_g, gdn_w_in, gdn_conv_w, gdn_a_log, gdn_dt_bias, gdn_norm_g, gdn_w_out, mlstm_w_in, mlstm_b_i, mlstm_b_f, mlstm_norm_g, mlstm_w_out, diff_w_in, diff_lambda, diff_norm_g, diff_w_out, swa_w_in, swa_sink, swa_w_out, router_w, router_b, moe_w_gu, moe_b_gu, moe_w_down, moe_b_down):
    nb, seq, d = x.shape
    lc = ctx.shape[1]
    depth = ada_w.shape[0]
    assert lc % TM == 0 and seq % TM == 0 and nb < 8
    nct = lc // TM
    xt = jnp.concatenate([ctx, x], axis=1)
    cond8 = jnp.zeros((8, d), F32).at[:nb].set(c).at[nb].set(c_ctx)
    mods_all = _ada_mods(cond8, ada_w, ada_b)
    cos, sin = _rope_tables(seq, lc)
    for i in range(depth):
        kind, j = i % 4, i // 4
        mods = mods_all[i]
        gm = norm_mix_g[i].reshape(1, d)
        gf = norm_ffn_g[i].reshape(1, d)
        rw = _pad_lanes(router_w[i])
        rb = _pad_lanes(router_b[i].reshape(1, -1), -jnp.inf)
        if kind == 0:
            w = gdn_w_in[j]
            main = w.shape[1] - 32
            p, gt, gc = _inproj(0, xt, gm, mods, w[:, :main].astype(BF16),
                                (_pad_lanes(w[:, main:]), _gate_consts(gdn_a_log[j], jnp.zeros_like(gdn_a_log[j])),
                                 _gate_consts(gdn_dt_bias[j], jnp.zeros_like(gdn_dt_bias[j]))), nct)
            q, k, v = _gdn_prep(p, jnp.pad(gdn_conv_w[j], ((0, 3), (0, 0))), nct)
            of, ob = _gdn_scan(q, k, v, gt, gc, nct)
            mixer_out = (of, ob, p, 3 * d, gdn_norm_g[j].reshape(1, -1))
            w_out, scale = gdn_w_out[j], 1.0
        elif kind == 1:
            w = mlstm_w_in[j]
            main = w.shape[1] - 32
            p, gt, gc = _inproj(1, xt, gm, mods, w[:, :main].astype(BF16),
                                (_pad_lanes(w[:, main:]), _gate_consts(mlstm_b_i[j], jnp.zeros_like(mlstm_b_i[j])),
                                 _pad_lanes(jnp.concatenate([jnp.zeros((1, 16), F32), mlstm_b_f[j].reshape(1, -1)], axis=1))),
                                nct)
            of, ob = _mlstm_scan(p, d // 2, d, gt, gc, nct)
            mixer_out = (of, ob, p, 2 * d, mlstm_norm_g[j].reshape(1, -1))
            w_out, scale = mlstm_w_out[j], 1.0
        elif kind == 2:
            lambda_init = 0.8 - 0.6 * math.exp(-0.3 * i)
            q, k, v = _inproj(2, xt, gm, mods, diff_w_in[j].astype(BF16), (cos, sin, (d, d, d)), nct)
            o = jnp.concatenate([_diff_attn(q[:, :lc], k[:, :lc], v[:, :lc], diff_lambda[j], lambda_init),
                                 _diff_attn(q[:, lc:], k, v, diff_lambda[j], lambda_init)], axis=1)
            mixer_out = (o, diff_norm_g[j].reshape(1, -1))
            w_out, scale = diff_w_out[j], 1.0 - lambda_init
        else:
            q, k, v = _inproj(3, xt, gm, mods, swa_w_in[j].astype(BF16), (cos, sin, (d, d // 4, d // 4)), nct)
            mixer_out = _swa_attn(q, k, v, _pad_lanes(swa_sink[j].reshape(1, -1)), nct, seq)
            w_out, scale = swa_w_out[j], 1.0
        x1, f, rinfo, counts = _outproj(kind, mixer_out, xt, mods, w_out.astype(BF16), gf, rw, rb, nct, scale)
        xt = _moe(f, rinfo, counts, x1, mods, i, moe_w_gu, moe_b_gu[i], moe_w_down, moe_b_down[i], nct)
    return _final_norm(xt, final_g.reshape(1, d), nct)
```
